```python
import jax, jax.numpy as jnp
from jax import lax
import numpy as np

D_MODEL = 2048
BATCH = 1
SEQ = 8192
DEPTH = 1
DEC_BATCH = 128
DEC_SEQ = 8
PAST_LEN = 16384
PAGE_SIZE = 128

MLA_HEADS = 16
Q_LORA = 512
KV_LORA = 512
NOPE_HD = 128
ROPE_HD = 64
V_HD = 128
MLA_W = MLA_HEADS * V_HD
MLA_SCALE = (NOPE_HD + ROPE_HD) ** -0.5
ROPE_THETA = 10000.0
FOX_HEADS = 16
FOX_KV_HEADS = 2
FOX_HD = 128
FOX_GROUP = FOX_HEADS // FOX_KV_HEADS
FOX_W = FOX_HEADS * FOX_HD
FOX_KV_W = FOX_KV_HEADS * FOX_HD
FOX_SCALE = FOX_HD ** -0.5
MEM_TOKENS = 256
MEM_HEADS = 4
MEM_HD = 128
MEM_W = MEM_HEADS * MEM_HD
MEM_SCALE = MEM_HD ** -0.5
N_BRANCH = 3
Q_BLOCK = 128
RMS_EPS = 1e-6
LN_EPS = 1e-5
DEEPNORM_ALPHA = (2 * DEPTH) ** 0.25
DEEPNORM_BETA = (8 * DEPTH) ** -0.25
IN_SPLITS = (Q_LORA, KV_LORA, ROPE_HD, MLA_W, FOX_W, FOX_KV_W, FOX_KV_W, FOX_HEADS, FOX_W, MEM_W, N_BRANCH * D_MODEL)
N_IN = sum(IN_SPLITS)

kernel_name = "mla_fox_memory_gated_hybrid_step"


def _rms_norm(x, g):
    xf = x.astype(jnp.float32)
    y = xf * lax.rsqrt(jnp.mean(xf * xf, axis=-1, keepdims=True) + RMS_EPS)
    return (y * g.astype(jnp.float32)).astype(x.dtype)


def _layer_norm(x, g, b):
    xf = x.astype(jnp.float32)
    mu = jnp.mean(xf, axis=-1, keepdims=True)
    var = jnp.mean(jnp.square(xf - mu), axis=-1, keepdims=True)
    y = (xf - mu) * lax.rsqrt(var + LN_EPS)
    return (y * g.astype(jnp.float32) + b.astype(jnp.float32)).astype(x.dtype)


def _rope(x, pos):
    half = ROPE_HD // 2
    inv = ROPE_THETA ** (-jnp.arange(half, dtype=jnp.float32) / half)
    ang = pos.astype(jnp.float32)[:, None] * inv[None, :]
    cos = jnp.cos(ang)[None, :, None, :]
    sin = jnp.sin(ang)[None, :, None, :]
    xf = x.astype(jnp.float32)
    x1, x2 = xf[..., :half], xf[..., half:]
    return jnp.concatenate([x1 * cos - x2 * sin, x1 * sin + x2 * cos], axis=-1).astype(x.dtype)


def _in_features(x, pos, w_in, b_fox_f, q_norm_g, kv_norm_g, w_uq, w_uk):
    B, T, _ = x.shape
    h = jnp.einsum('btd,dn->btn', x, w_in)
    parts = []
    off = 0
    for sz in IN_SPLITS:
        parts.append(h[..., off:off + sz])
        off += sz
    q_lat, c_kv, k_r, mla_gate, fq, fk, fv, f_logit, fox_gate, mq, br = parts
    q = jnp.einsum('btq,qn->btn', _rms_norm(q_lat, q_norm_g), w_uq).reshape(B, T, MLA_HEADS, NOPE_HD + ROPE_HD)
    q_abs = jnp.einsum('bthd,lhd->bthl', q[..., :NOPE_HD], w_uk)
    q_rope = _rope(q[..., NOPE_HD:], pos)
    c_kv = _rms_norm(c_kv, kv_norm_g)
    k_rope = _rope(k_r[:, :, None, :], pos)[:, :, 0, :]
    fox_q = fq.reshape(B, T, FOX_HEADS, FOX_HD)
    fox_k = fk.reshape(B, T, FOX_KV_HEADS, FOX_HD)
    fox_v = fv.reshape(B, T, FOX_KV_HEADS, FOX_HD)
    log_f = jax.nn.log_sigmoid((f_logit + b_fox_f).astype(jnp.float32))
    mem_q = mq.reshape(B, T, MEM_HEADS, MEM_HD)
    br_gates = jax.nn.sigmoid(br).reshape(B, T, N_BRANCH, D_MODEL)
    return q_abs, q_rope, c_kv, k_rope, mla_gate, fox_q, fox_k, fox_v, log_f, fox_gate, mem_q, br_gates


def _mla_attend(q_abs, q_rope, ckv, krope, q_pos, k_pos, w_uv):
    s = jnp.einsum('bthl,bsl->bhts', q_abs, ckv) + jnp.einsum('bthr,bsr->bhts', q_rope, krope)
    s = s.astype(jnp.float32) * MLA_SCALE
    s = jnp.where(k_pos[None, :] <= q_pos[:, None], s, -jnp.inf)
    p = jax.nn.softmax(s, axis=-1).astype(ckv.dtype)
    o_lat = jnp.einsum('bhts,bsl->bthl', p, ckv)
    o = jnp.einsum('bthl,lhd->bthd', o_lat, w_uv)
    return o.reshape(o.shape[0], o.shape[1], MLA_W)


def _fox_attend(q, k, v, fq, fk, q_pos, k_pos):
    B, T = q.shape[0], q.shape[1]
    S = k.shape[1]
    qg = q.reshape(B, T, FOX_KV_HEADS, FOX_GROUP, FOX_HD)
    s = jnp.einsum('btkgd,bskd->bkgts', qg, k).astype(jnp.float32) * FOX_SCALE
    fq_ = fq.reshape(B, T, FOX_KV_HEADS, FOX_GROUP).transpose(0, 2, 3, 1)
    fk_ = fk.reshape(B, S, FOX_KV_HEADS, FOX_GROUP).transpose(0, 2, 3, 1)
    s = s + (fq_[..., :, None] - fk_[..., None, :])
    s = jnp.where(k_pos[None, :] <= q_pos[:, None], s, -jnp.inf)
    p = jax.nn.softmax(s, axis=-1).astype(v.dtype)
    o = jnp.einsum('bkgts,bskd->btkgd', p, v)
    return o.reshape(B, T, FOX_W)


def _mem_attend(q, k, v):
    s = jnp.einsum('bthd,bshd->bhts', q, k).astype(jnp.float32) * MEM_SCALE
    p = jax.nn.softmax(s, axis=-1).astype(v.dtype)
    o = jnp.einsum('bhts,bshd->bthd', p, v)
    return o.reshape(o.shape[0], o.shape[1], MEM_W)


def _merge(x, o_a, mla_gate, o_f, fox_gate, o_m, br_gates, w_br_mla, w_br_fox, w_br_mem, w_out, ln_g, ln_b):
    a = jnp.einsum('btn,nd->btd', o_a * jax.nn.silu(mla_gate), w_br_mla)
    f = jnp.einsum('btn,nd->btd', o_f * jax.nn.silu(fox_gate), w_br_fox)
    m = jnp.einsum('btn,nd->btd', o_m, w_br_mem)
    merged = br_gates[:, :, 0, :] * a + br_gates[:, :, 1, :] * f + br_gates[:, :, 2, :] * m
    y = jnp.einsum('btd,de->bte', merged, w_out)
    return _layer_norm(DEEPNORM_ALPHA * x + y, ln_g, ln_b)


def _prompt_layer(x, mem, lw):
    (w_in, b_fox_f, q_norm_g, kv_norm_g, w_uq, w_uk, w_uv, w_mem_kv,
     w_br_mla, w_br_fox, w_br_mem, w_out, ln_g, ln_b) = lw
    B, T, _ = x.shape
    pos = jnp.arange(T, dtype=jnp.int32)
    (q_abs, q_rope, c_kv, k_rope, mla_gate, fox_q, fox_k, fox_v,
     log_f, fox_gate, mem_q, br_gates) = _in_features(x, pos, w_in, b_fox_f, q_norm_g, kv_norm_g, w_uq, w_uk)
    mkv = jnp.einsum('bmd,dn->bmn', mem, w_mem_kv)
    mem_k = mkv[..., :MEM_W].reshape(B, mem.shape[1], MEM_HEADS, MEM_HD)
    mem_v = mkv[..., MEM_W:].reshape(B, mem.shape[1], MEM_HEADS, MEM_HD)
    F = jnp.cumsum(log_f, axis=1)
    n_blk = T // Q_BLOCK

    def block(i):
        st = i * Q_BLOCK
        sl = lambda a: lax.dynamic_slice_in_dim(a, st, Q_BLOCK, axis=1)
        q_pos = st + jnp.arange(Q_BLOCK, dtype=jnp.int32)
        o_a = _mla_attend(sl(q_abs), sl(q_rope), c_kv, k_rope, q_pos, pos, w_uv)
        o_f = _fox_attend(sl(fox_q), fox_k, fox_v, sl(F), F, q_pos, pos)
        return o_a, o_f

    o_a, o_f = lax.map(block, jnp.arange(n_blk, dtype=jnp.int32))
    o_a = o_a.transpose(1, 0, 2, 3).reshape(B, T, MLA_W)
    o_f = o_f.transpose(1, 0, 2, 3).reshape(B, T, FOX_W)
    o_m = _mem_attend(mem_q, mem_k, mem_v)
    y = _merge(x, o_a, mla_gate, o_f, fox_gate, o_m, br_gates, w_br_mla, w_br_fox, w_br_mem, w_out, ln_g, ln_b)
    return y, (c_kv, k_rope, fox_k, fox_v, log_f, mem_k, mem_v)


def _sample_layer(x, layer, cache_mla_ckv, cache_mla_krope, cache_fox_k, cache_fox_v, cache_fox_logf,
                  cache_mem_k, cache_mem_v, page_table, lw):
    (w_in, b_fox_f, q_norm_g, kv_norm_g, w_uq, w_uk, w_uv, w_mem_kv,
     w_br_mla, w_br_fox, w_br_mem, w_out, ln_g, ln_b) = lw
    T = x.shape[1]
    past = page_table.shape[1] * cache_mla_ckv.shape[2]
    pos = past + jnp.arange(T, dtype=jnp.int32)
    k_pos = jnp.arange(past + T, dtype=jnp.int32)
    (q_abs, q_rope, c_kv, k_rope, mla_gate, fox_q, fox_k, fox_v,
     log_f, fox_gate, mem_q, br_gates) = _in_features(x, pos, w_in, b_fox_f, q_norm_g, kv_norm_g, w_uq, w_uk)

    def one(args):
        pages, qa, qr, ckv_n, kr_n, fq, fk_n, fv_n, lf_n = args
        ckv = jnp.concatenate([cache_mla_ckv[layer, pages].reshape(past, KV_LORA), ckv_n], axis=0)
        kr = jnp.concatenate([cache_mla_krope[layer, pages].reshape(past, ROPE_HD), kr_n], axis=0)
        fk = jnp.concatenate([cache_fox_k[layer, pages].reshape(past, FOX_KV_HEADS, FOX_HD), fk_n], axis=0)
        fv = jnp.concatenate([cache_fox_v[layer, pages].reshape(past, FOX_KV_HEADS, FOX_HD), fv_n], axis=0)
        lf = jnp.concatenate([cache_fox_logf[layer, pages].reshape(past, FOX_HEADS).astype(jnp.float32), lf_n], axis=0)
        F = jnp.cumsum(lf, axis=0)
        o_a = _mla_attend(qa[None], qr[None], ckv[None], kr[None], pos, k_pos, w_uv)[0]
        o_f = _fox_attend(fq[None], fk[None], fv[None], F[None, past:], F[None], pos, k_pos)[0]
        return o_a, o_f

    o_a, o_f = lax.map(one, (page_table, q_abs, q_rope, c_kv, k_rope, fox_q, fox_k, fox_v, log_f))
    o_m = _mem_attend(mem_q, cache_mem_k[layer], cache_mem_v[layer])
    y = _merge(x, o_a, mla_gate, o_f, fox_gate, o_m, br_gates, w_br_mla, w_br_fox, w_br_mem, w_out, ln_g, ln_b)
    return y, (c_kv, k_rope, fox_k, fox_v, log_f)


def setup_inputs(seed: int = 0) -> dict:
    key = jax.random.key(seed)
    ks = jax.random.split(key, 32)
    n_pages = PAST_LEN // PAGE_SIZE
    n_used = DEC_BATCH * n_pages
    n_pool = n_used + n_used // 4

    def nrm(k, shape, scale=1.0):
        return scale * jax.random.normal(k, shape, jnp.float32)

    inp = {}
    inp['x_prompt'] = nrm(ks[0], (BATCH, SEQ, D_MODEL))
    inp['x_sample'] = nrm(ks[1], (DEC_BATCH, DEC_SEQ, D_MODEL))
    inp['mem_prompt'] = nrm(ks[2], (BATCH, MEM_TOKENS, D_MODEL))
    inp['cache_mla_ckv'] = nrm(ks[3], (DEPTH, n_pool, PAGE_SIZE, KV_LORA))
    inp['cache_mla_krope'] = nrm(ks[4], (DEPTH, n_pool, PAGE_SIZE, ROPE_HD))
    inp['cache_fox_k'] = nrm(ks[5], (DEPTH, n_pool, PAGE_SIZE, FOX_KV_HEADS, FOX_HD))
    inp['cache_fox_v'] = nrm(ks[6], (DEPTH, n_pool, PAGE_SIZE, FOX_KV_HEADS, FOX_HD))
    inp['cache_fox_logf'] = jax.nn.log_sigmoid(nrm(ks[7], (DEPTH, n_pool, PAGE_SIZE, FOX_HEADS)))
    inp['cache_mem_k'] = nrm(ks[8], (DEPTH, DEC_BATCH, MEM_TOKENS, MEM_HEADS, MEM_HD))
    inp['cache_mem_v'] = nrm(ks[9], (DEPTH, DEC_BATCH, MEM_TOKENS, MEM_HEADS, MEM_HD))
    inp['page_table'] = jax.random.permutation(ks[10], n_pool)[:n_used].reshape(DEC_BATCH, n_pages).astype(jnp.int32)
    inp['w_in'] = nrm(ks[11], (DEPTH, D_MODEL, N_IN), D_MODEL ** -0.5)
    inp['b_fox_f'] = nrm(ks[12], (DEPTH, FOX_HEADS), 0.1)
    inp['q_norm_g'] = 1.0 + nrm(ks[13], (DEPTH, Q_LORA), 0.05)
    inp['kv_norm_g'] = 1.0 + nrm(ks[14], (DEPTH, KV_LORA), 0.05)
    inp['w_uq'] = nrm(ks[15], (DEPTH, Q_LORA, MLA_HEADS * (NOPE_HD + ROPE_HD)), Q_LORA ** -0.5)
    inp['w_uk'] = nrm(ks[16], (DEPTH, KV_LORA, MLA_HEADS, NOPE_HD), KV_LORA ** -0.5)
    inp['w_uv'] = nrm(ks[17], (DEPTH, KV_LORA, MLA_HEADS, V_HD), KV_LORA ** -0.5)
    inp['w_mem_kv'] = nrm(ks[18], (DEPTH, D_MODEL, 2 * MEM_W), D_MODEL ** -0.5)
    inp['w_br_mla'] = nrm(ks[19], (DEPTH, MLA_W, D_MODEL), DEEPNORM_BETA * MLA_W ** -0.5)
    inp['w_br_fox'] = nrm(ks[20], (DEPTH, FOX_W, D_MODEL), DEEPNORM_BETA * FOX_W ** -0.5)
    inp['w_br_mem'] = nrm(ks[21], (DEPTH, MEM_W, D_MODEL), DEEPNORM_BETA * MEM_W ** -0.5)
    inp['w_out'] = nrm(ks[22], (DEPTH, D_MODEL, D_MODEL), DEEPNORM_BETA * D_MODEL ** -0.5)
    inp['ln_g'] = 1.0 + nrm(ks[23], (DEPTH, D_MODEL), 0.05)
    inp['ln_b'] = nrm(ks[24], (DEPTH, D_MODEL), 0.02)
    return inp


def reference(x_prompt, x_sample, mem_prompt, cache_mla_ckv, cache_mla_krope, cache_fox_k, cache_fox_v,
              cache_fox_logf, cache_mem_k, cache_mem_v, page_table, w_in, b_fox_f, q_norm_g, kv_norm_g,
              w_uq, w_uk, w_uv, w_mem_kv, w_br_mla, w_br_fox, w_br_mem, w_out, ln_g, ln_b):
    hp, hs = x_prompt, x_sample
    p_st, s_st = [], []
    for layer in range(DEPTH):
        lw = (w_in[layer], b_fox_f[layer], q_norm_g[layer], kv_norm_g[layer], w_uq[layer], w_uk[layer],
              w_uv[layer], w_mem_kv[layer], w_br_mla[layer], w_br_fox[layer], w_br_mem[layer],
              w_out[layer], ln_g[layer], ln_b[layer])
        hp, sp = _prompt_layer(hp, mem_prompt, lw)
        hs, ss = _sample_layer(hs, layer, cache_mla_ckv, cache_mla_krope, cache_fox_k, cache_fox_v,
                               cache_fox_logf, cache_mem_k, cache_mem_v, page_table, lw)
        p_st.append(sp)
        s_st.append(ss)
    p_ckv = jnp.stack([s[0] for s in p_st])
    p_krope = jnp.stack([s[1] for s in p_st])
    p_fox_k = jnp.stack([s[2] for s in p_st])
    p_fox_v = jnp.stack([s[3] for s in p_st])
    p_fox_logf = jnp.stack([s[4] for s in p_st])
    p_mem_k = jnp.stack([s[5] for s in p_st])
    p_mem_v = jnp.stack([s[6] for s in p_st])
    s_ckv = jnp.stack([s[0] for s in s_st])
    s_krope = jnp.stack([s[1] for s in s_st])
    s_fox_k = jnp.stack([s[2] for s in s_st])
    s_fox_v = jnp.stack([s[3] for s in s_st])
    s_fox_logf = jnp.stack([s[4] for s in s_st])
    return (hp, hs, p_ckv, p_krope, p_fox_k, p_fox_v, p_fox_logf, p_mem_k, p_mem_v,
            s_ckv, s_krope, s_fox_k, s_fox_v, s_fox_logf)
```

```python
import functools

import jax
import jax.numpy as jnp
from jax import lax
from jax.experimental import pallas as pl
from jax.experimental.pallas import tpu as pltpu

BF = jnp.bfloat16
F32 = jnp.float32

D_MODEL = 2048
MLA_HEADS = 16
Q_LORA = 512
KV_LORA = 512
NOPE_HD = 128
ROPE_HD = 64
V_HD = 128
MLA_W = MLA_HEADS * V_HD
MLA_SCALE = (NOPE_HD + ROPE_HD) ** -0.5
ROPE_THETA = 10000.0
FOX_HEADS = 16
FOX_KV_HEADS = 2
FOX_HD = 128
FOX_GROUP = FOX_HEADS // FOX_KV_HEADS
FOX_W = FOX_HEADS * FOX_HD
FOX_KV_W = FOX_KV_HEADS * FOX_HD
FOX_SCALE = FOX_HD ** -0.5
MEM_HEADS = 4
MEM_HD = 128
MEM_W = MEM_HEADS * MEM_HD
MEM_SCALE = MEM_HD ** -0.5
N_BRANCH = 3
RMS_EPS = 1e-6
LN_EPS = 1e-5

NEG = -1e30
VMEM_LIMIT = 56 * 1024 * 1024
PAGES_PER_STEP = 8

A_QLAT, A_CKV, A_KR, A_KRS, A_FK, A_FV, A_FLOG, A_END = 0, 512, 1024, 1088, 1152, 1408, 1664, 1792
B_GA, B_GF, B_BR, B_FQ, B_MQ, B_END = 0, 2048, 4096, 10240, 12288, 12800


def _pick(n, target, align=8):
    if n <= target:
        return n
    for d in range(target, 0, -1):
        if n % d == 0 and d % align == 0:
            return d
    raise ValueError((n, target, align))


def _params(n_axes):
    return pltpu.CompilerParams(dimension_semantics=("arbitrary",) * n_axes, vmem_limit_bytes=VMEM_LIMIT)


def _dot_nt(a, b):
    return lax.dot_general(a, b, (((1,), (1,)), ((), ())), preferred_element_type=F32)


def _dot(a, b):
    return jnp.dot(a, b, preferred_element_type=F32)


def _mm_kernel(a_ref, b_ref, o_ref, abf_ref):
    @pl.when(pl.program_id(1) == 0)
    def _():
        abf_ref[...] = a_ref[...].astype(BF)

    o_ref[...] = _dot(abf_ref[...], b_ref[...]).astype(o_ref.dtype)


def _mm(a, b, out_dtype, tm, tn, name):
    M, K = a.shape
    N = b.shape[1]
    tm = _pick(M, tm)
    tn = _pick(N, tn, 128)
    return pl.pallas_call(
        _mm_kernel,
        grid=(M // tm, N // tn),
        in_specs=[pl.BlockSpec((tm, K), lambda i, j: (i, 0)), pl.BlockSpec((K, tn), lambda i, j: (0, j))],
        out_specs=pl.BlockSpec((tm, tn), lambda i, j: (i, j)),
        out_shape=jax.ShapeDtypeStruct((M, N), out_dtype),
        scratch_shapes=[pltpu.VMEM((tm, K), BF)],
        compiler_params=_params(2),
        name=name,
    )(a, b)


def _headwise_kernel(a_ref, w_ref, o_ref):
    o_ref[...] = _dot(a_ref[...], w_ref[...]).astype(o_ref.dtype)


def _headwise_mm(a, w, out_dtype, tm, name):
    M = a.shape[0]
    H, K, N = w.shape
    tm = _pick(M, tm)
    return pl.pallas_call(
        _headwise_kernel,
        grid=(H, M // tm),
        in_specs=[pl.BlockSpec((tm, K), lambda h, i: (i, h)), pl.BlockSpec((None, K, N), lambda h, i: (h, 0, 0))],
        out_specs=pl.BlockSpec((tm, N), lambda h, i: (i, h)),
        out_shape=jax.ShapeDtypeStruct((M, H * N), out_dtype),
        compiler_params=_params(2),
        name=name,
    )(a, w)


def _mla_prompt_kernel(qn_ref, qr_ref, kn_ref, kr_ref, v_ref, o_ref, m_ref, l_ref, acc_ref, *, tq):
    i = pl.program_id(1)
    row = lax.broadcasted_iota(jnp.int32, (tq, tq), 0)
    col = lax.broadcasted_iota(jnp.int32, (tq, tq), 1)
    for hh in range(2):
        qn = qn_ref[:, hh * NOPE_HD:(hh + 1) * NOPE_HD]
        qr = qr_ref[:, hh * ROPE_HD:(hh + 1) * ROPE_HD]

        def scores(j, qn=qn, qr=qr, hh=hh):
            kn = kn_ref[pl.ds(j * tq, tq), hh * NOPE_HD:(hh + 1) * NOPE_HD]
            kr = kr_ref[pl.ds(j * tq, tq), :]
            return (_dot_nt(qn, kn) + _dot_nt(qr, kr)) * MLA_SCALE

        def values(j, hh=hh):
            return v_ref[pl.ds(j * tq, tq), hh * V_HD:(hh + 1) * V_HD]

        s = jnp.where(col <= row, scores(i), NEG)
        m = jnp.max(s, axis=1, keepdims=True)
        p = jnp.exp(s - m)
        m_ref[...] = m
        l_ref[...] = jnp.sum(p, axis=1, keepdims=True)
        acc_ref[...] = _dot(p.astype(BF), values(i))

        def body(j, carry, scores=scores, values=values):
            s = scores(j)
            m_prev = m_ref[...]
            m_new = jnp.maximum(m_prev, jnp.max(s, axis=1, keepdims=True))
            alpha = jnp.exp(m_prev - m_new)
            p = jnp.exp(s - m_new)
            l_ref[...] = alpha * l_ref[...] + jnp.sum(p, axis=1, keepdims=True)
            acc_ref[...] = alpha * acc_ref[...] + _dot(p.astype(BF), values(j))
            m_ref[...] = m_new
            return carry

        lax.fori_loop(0, i, body, 0)
        o_ref[:, hh * V_HD:(hh + 1) * V_HD] = (acc_ref[...] / l_ref[...]).astype(o_ref.dtype)


def _mla_prompt(qn, qr, kn, kr, v):
    T = qn.shape[0]
    tq = _pick(T, 512)
    return pl.pallas_call(
        functools.partial(_mla_prompt_kernel, tq=tq),
        grid=(MLA_HEADS // 2, T // tq),
        in_specs=[
            pl.BlockSpec((tq, 2 * NOPE_HD), lambda h, i: (i, h)),
            pl.BlockSpec((tq, 2 * ROPE_HD), lambda h, i: (i, h)),
            pl.BlockSpec((T, 2 * NOPE_HD), lambda h, i: (0, h)),
            pl.BlockSpec((T, ROPE_HD), lambda h, i: (0, 0)),
            pl.BlockSpec((T, 2 * V_HD), lambda h, i: (0, h)),
        ],
        out_specs=pl.BlockSpec((tq, 2 * V_HD), lambda h, i: (i, h)),
        out_shape=jax.ShapeDtypeStruct((T, MLA_W), BF),
        scratch_shapes=[pltpu.VMEM((tq, 1), F32), pltpu.VMEM((tq, 1), F32), pltpu.VMEM((tq, V_HD), F32)],
        compiler_params=_params(2),
        name="mla_prompt",
    )(qn, qr, kn, kr, v)


def _fox_prompt_kernel(q_ref, k_ref, v_ref, fq_ref, fk_ref, o_ref, m_ref, l_ref, acc_ref, *, tq):
    g = pl.program_id(0)
    i = pl.program_id(1)
    row = lax.broadcasted_iota(jnp.int32, (tq, tq), 0)
    col = lax.broadcasted_iota(jnp.int32, (tq, tq), 1)
    fq_all = fq_ref[...]
    for hh in range(FOX_GROUP):
        q = q_ref[:, hh * FOX_HD:(hh + 1) * FOX_HD]
        fq = fq_all[:, hh:hh + 1]

        def scores(j, q=q, fq=fq, hh=hh):
            k = k_ref[pl.ds(j * tq, tq), :]
            fk = fk_ref[hh, pl.ds(j, 1), :]
            return _dot_nt(q, k) * FOX_SCALE + (fq - fk)

        def values(j):
            return v_ref[pl.ds(j * tq, tq), :]

        s = jnp.where(col <= row, scores(i), NEG)
        m = jnp.max(s, axis=1, keepdims=True)
        p = jnp.exp(s - m)
        m_ref[...] = m
        l_ref[...] = jnp.sum(p, axis=1, keepdims=True)
        acc_ref[...] = _dot(p.astype(BF), values(i))

        def body(j, carry, scores=scores):
            s = scores(j)
            m_prev = m_ref[...]
            m_new = jnp.maximum(m_prev, jnp.max(s, axis=1, keepdims=True))
            alpha = jnp.exp(m_prev - m_new)
            p = jnp.exp(s - m_new)
            l_ref[...] = alpha * l_ref[...] + jnp.sum(p, axis=1, keepdims=True)
            acc_ref[...] = alpha * acc_ref[...] + _dot(p.astype(BF), values(j))
            m_ref[...] = m_new
            return carry

        lax.fori_loop(0, i, body, 0)
        o_ref[:, hh * FOX_HD:(hh + 1) * FOX_HD] = (acc_ref[...] / l_ref[...]).astype(o_ref.dtype)
    del g


def _fox_prompt(hb, k, v, fcum, T):
    tq = _pick(T, 512)
    gw = FOX_GROUP * FOX_HD
    fk3 = fcum.T.reshape(FOX_KV_HEADS, FOX_GROUP, T // tq, tq)
    fq3 = fcum.reshape(T, FOX_KV_HEADS, FOX_GROUP).transpose(1, 0, 2)
    return pl.pallas_call(
        functools.partial(_fox_prompt_kernel, tq=tq),
        grid=(FOX_KV_HEADS, T // tq),
        in_specs=[
            pl.BlockSpec((tq, gw), lambda g, i: (i, B_FQ // gw + g)),
            pl.BlockSpec((T, FOX_HD), lambda g, i: (0, g)),
            pl.BlockSpec((T, FOX_HD), lambda g, i: (0, g)),
            pl.BlockSpec((None, tq, FOX_GROUP), lambda g, i: (g, i, 0)),
            pl.BlockSpec((None, FOX_GROUP, T // tq, tq), lambda g, i: (g, 0, 0, 0)),
        ],
        out_specs=pl.BlockSpec((tq, gw), lambda g, i: (i, g)),
        out_shape=jax.ShapeDtypeStruct((T, FOX_W), BF),
        scratch_shapes=[pltpu.VMEM((tq, 1), F32), pltpu.VMEM((tq, 1), F32), pltpu.VMEM((tq, FOX_HD), F32)],
        compiler_params=_params(2),
        name="fox_prompt",
    )(hb, k, v, fq3, fk3)


def _mem_kernel(q_ref, k_ref, v_ref, o_ref):
    for h in range(MEM_HEADS):
        sl = slice(h * MEM_HD, (h + 1) * MEM_HD)
        q = q_ref[:, sl].astype(BF)
        k = k_ref[:, sl].astype(BF)
        v = v_ref[:, sl].astype(BF)
        s = _dot_nt(q, k) * MEM_SCALE
        m = jnp.max(s, axis=1, keepdims=True)
        p = jnp.exp(s - m)
        l = jnp.sum(p, axis=1, keepdims=True)
        o_ref[:, sl] = (_dot(p.astype(BF), v) / l).astype(o_ref.dtype)


def _mem_attend(q, k, v, tq, out_dtype, name):
    B, Tq, _ = q.shape
    M = k.shape[1]
    tq = _pick(Tq, tq)
    return pl.pallas_call(
        _mem_kernel,
        grid=(B, Tq // tq),
        in_specs=[
            pl.BlockSpec((None, tq, MEM_W), lambda b, i: (b, i, 0)),
            pl.BlockSpec((None, M, MEM_W), lambda b, i: (b, 0, 0)),
            pl.BlockSpec((None, M, MEM_W), lambda b, i: (b, 0, 0)),
        ],
        out_specs=pl.BlockSpec((None, tq, MEM_W), lambda b, i: (b, i, 0)),
        out_shape=jax.ShapeDtypeStruct((B, Tq, MEM_W), out_dtype),
        compiler_params=_params(2),
        name=name,
    )(q, k, v)


def _decode_kernel(pt_ref, qa_ref, qr_ref, qf_ref, *rest, P, n_steps, page, t_new):
    ckv_refs, kr_refs, fk_refs, fv_refs, fl_refs = (rest[k * P:(k + 1) * P] for k in range(5))
    ckvn_ref, krn_ref, fkn_ref, fvn_ref, fln_ref = rest[5 * P:5 * P + 5]
    oa_ref, of_ref = rest[5 * P + 5:5 * P + 7]
    kall, krall, fkall, fvall, bias, fbase, m_a, l_a, acc_a, m_f, l_f, acc_f = rest[5 * P + 7:]
    del pt_ref
    j = pl.program_id(1)
    n_rows_a = t_new * MLA_HEADS
    n_rows_f = t_new * FOX_GROUP

    @pl.when(j == 0)
    def _():
        m_a[...] = jnp.full(m_a.shape, NEG, F32)
        l_a[...] = jnp.zeros(l_a.shape, F32)
        acc_a[...] = jnp.zeros(acc_a.shape, F32)
        m_f[...] = jnp.full(m_f.shape, NEG, F32)
        l_f[...] = jnp.zeros(l_f.shape, F32)
        acc_f[...] = jnp.zeros(acc_f.shape, F32)
        fbase[...] = jnp.zeros(fbase.shape, F32)

    def attend(k, kr, fk, fv, b, mask_a, mask_f):
        s = (_dot_nt(qa_ref[...], k) + _dot_nt(qr_ref[...], kr)) * MLA_SCALE
        if mask_a is not None:
            s = jnp.where(mask_a, s, NEG)
        m_prev = m_a[...]
        m_new = jnp.maximum(m_prev, jnp.max(s, axis=1, keepdims=True))
        alpha = jnp.exp(m_prev - m_new)
        p = jnp.exp(s - m_new)
        l_a[...] = alpha * l_a[...] + jnp.sum(p, axis=1, keepdims=True)
        acc_a[...] = alpha * acc_a[...] + _dot(p.astype(BF), k)
        m_a[...] = m_new
        for g in range(FOX_KV_HEADS):
            sl = slice(g * FOX_HD, (g + 1) * FOX_HD)
            bg = b[g * FOX_GROUP:(g + 1) * FOX_GROUP, :]
            s = _dot_nt(qf_ref[g], fk[:, sl]) * FOX_SCALE - jnp.concatenate([bg] * t_new, axis=0)
            if mask_f is not None:
                s = jnp.where(mask_f, s, NEG)
            m_prev = m_f[g]
            m_new = jnp.maximum(m_prev, jnp.max(s, axis=1, keepdims=True))
            alpha = jnp.exp(m_prev - m_new)
            p = jnp.exp(s - m_new)
            l_f[g] = alpha * l_f[g] + jnp.sum(p, axis=1, keepdims=True)
            acc_f[g] = alpha * acc_f[g] + _dot(p.astype(BF), fv[:, sl])
            m_f[g] = m_new

    for p in range(P):
        rows = slice(p * page, (p + 1) * page)
        kall[rows, :] = ckv_refs[p][...].astype(BF)
        krall[rows, :] = kr_refs[p][...].astype(BF)
        fkall[rows, :] = fk_refs[p][...].astype(BF)
        fvall[rows, :] = fv_refs[p][...].astype(BF)
        fl = fl_refs[p][...]
        base = fbase[...]
        bias[:, rows] = base + fl
        fbase[...] = base + jnp.broadcast_to(fl[:, page - 1:page], base.shape)
    attend(kall[...], krall[...], fkall[...], fvall[...], bias[...], None, None)

    @pl.when(j == n_steps - 1)
    def _():
        col_a = lax.broadcasted_iota(jnp.int32, (n_rows_a, page), 1)
        tok_a = lax.broadcasted_iota(jnp.int32, (n_rows_a, page), 0) // MLA_HEADS
        col_f = lax.broadcasted_iota(jnp.int32, (n_rows_f, page), 1)
        tok_f = lax.broadcasted_iota(jnp.int32, (n_rows_f, page), 0) // FOX_GROUP
        attend(ckvn_ref[...], krn_ref[...], fkn_ref[...], fvn_ref[...], fbase[...] + fln_ref[...],
               col_a <= tok_a, col_f <= tok_f)
        oa_ref[...] = (acc_a[...] / l_a[...]).astype(oa_ref.dtype)
        for g in range(FOX_KV_HEADS):
            of_ref[g] = (acc_f[g] / l_f[g]).astype(of_ref.dtype)


def _decode(page_table, qa, qr, qf, ckv_pool, kr_pool, fk_pool, fv_pool, fl_pool, ckv_new, kr_new, fk_new, fv_new,
            fl_new):
    B, n_pages = page_table.shape
    page = ckv_pool.shape[1]
    P = _pick(n_pages, PAGES_PER_STEP, 1)
    n_steps = n_pages // P
    t_new = qa.shape[1] // MLA_HEADS
    n_rows_a = qa.shape[1]
    n_rows_f = qf.shape[2]

    def page_spec(width, p):
        return pl.BlockSpec((None, page, width), lambda b, j, pt, p=p: (pt[b * n_pages + j * P + p], 0, 0))

    def fl_spec(p):
        return pl.BlockSpec((None, FOX_HEADS, page), lambda b, j, pt, p=p: (pt[b * n_pages + j * P + p], 0, 0))

    def batch_spec(*tail):
        return pl.BlockSpec((None,) + tail, lambda b, j, pt: (b,) + (0,) * len(tail))

    in_specs = [batch_spec(n_rows_a, KV_LORA), batch_spec(n_rows_a, ROPE_HD), batch_spec(FOX_KV_HEADS, n_rows_f, FOX_HD)]
    in_specs += [page_spec(KV_LORA, p) for p in range(P)]
    in_specs += [page_spec(ROPE_HD, p) for p in range(P)]
    in_specs += [page_spec(FOX_KV_W, p) for p in range(P)]
    in_specs += [page_spec(FOX_KV_W, p) for p in range(P)]
    in_specs += [fl_spec(p) for p in range(P)]
    in_specs += [batch_spec(page, KV_LORA), batch_spec(page, ROPE_HD), batch_spec(page, FOX_KV_W),
                 batch_spec(page, FOX_KV_W), batch_spec(FOX_HEADS, page)]
    tk = P * page
    scratch = [
        pltpu.VMEM((tk, KV_LORA), BF), pltpu.VMEM((tk, ROPE_HD), BF), pltpu.VMEM((tk, FOX_KV_W), BF),
        pltpu.VMEM((tk, FOX_KV_W), BF), pltpu.VMEM((FOX_HEADS, tk), F32), pltpu.VMEM((FOX_HEADS, page), F32),
        pltpu.VMEM((n_rows_a, 1), F32), pltpu.VMEM((n_rows_a, 1), F32), pltpu.VMEM((n_rows_a, KV_LORA), F32),
        pltpu.VMEM((FOX_KV_HEADS, n_rows_f, 1), F32), pltpu.VMEM((FOX_KV_HEADS, n_rows_f, 1), F32),
        pltpu.VMEM((FOX_KV_HEADS, n_rows_f, FOX_HD), F32),
    ]
    grid_spec = pltpu.PrefetchScalarGridSpec(
        num_scalar_prefetch=1,
        grid=(B, n_steps),
        in_specs=in_specs,
        out_specs=[batch_spec(n_rows_a, KV_LORA), batch_spec(FOX_KV_HEADS, n_rows_f, FOX_HD)],
        scratch_shapes=scratch,
    )
    return pl.pallas_call(
        functools.partial(_decode_kernel, P=P, n_steps=n_steps, page=page, t_new=t_new),
        grid_spec=grid_spec,
        out_shape=[jax.ShapeDtypeStruct((B, n_rows_a, KV_LORA), BF),
                   jax.ShapeDtypeStruct((B, FOX_KV_HEADS, n_rows_f, FOX_HD), BF)],
        compiler_params=_params(2),
        name="decode",
    )(page_table.reshape(-1), qa, qr, qf,
      *([ckv_pool] * P), *([kr_pool] * P), *([fk_pool] * P), *([fv_pool] * P), *([fl_pool] * P),
      ckv_new, kr_new, fk_new, fv_new, fl_new)


def _merge_kernel(oa_ref, of_ref, om_ref, ga_ref, gf_ref, b0_ref, b1_ref, b2_ref, wa_ref, wf_ref, wm_ref, o_ref,
                  acta_ref, actf_ref):
    @pl.when(pl.program_id(1) == 0)
    def _():
        ga = ga_ref[...].astype(F32)
        gf = gf_ref[...].astype(F32)
        acta_ref[...] = (oa_ref[...].astype(F32) * (ga * jax.nn.sigmoid(ga))).astype(BF)
        actf_ref[...] = (of_ref[...].astype(F32) * (gf * jax.nn.sigmoid(gf))).astype(BF)

    a = _dot(acta_ref[...], wa_ref[...])
    f = _dot(actf_ref[...], wf_ref[...])
    m = _dot(om_ref[...], wm_ref[...])
    merged = (jax.nn.sigmoid(b0_ref[...].astype(F32)) * a + jax.nn.sigmoid(b1_ref[...].astype(F32)) * f
              + jax.nn.sigmoid(b2_ref[...].astype(F32)) * m)
    o_ref[...] = merged.astype(o_ref.dtype)


def _merge(oa, of, om, hb, wa, wf, wm):
    M = oa.shape[0]
    tm = _pick(M, 512)
    tn = 512
    nb = D_MODEL // tn
    row = lambda c: (lambda i, j: (i, c))
    return pl.pallas_call(
        _merge_kernel,
        grid=(M // tm, nb),
        in_specs=[
            pl.BlockSpec((tm, MLA_W), row(0)),
            pl.BlockSpec((tm, FOX_W), row(0)),
            pl.BlockSpec((tm, MEM_W), row(0)),
            pl.BlockSpec((tm, MLA_W), row(B_GA // MLA_W)),
            pl.BlockSpec((tm, FOX_W), row(B_GF // FOX_W)),
            pl.BlockSpec((tm, tn), lambda i, j: (i, B_BR // tn + j)),
            pl.BlockSpec((tm, tn), lambda i, j: (i, B_BR // tn + nb + j)),
            pl.BlockSpec((tm, tn), lambda i, j: (i, B_BR // tn + 2 * nb + j)),
            pl.BlockSpec((MLA_W, tn), lambda i, j: (0, j)),
            pl.BlockSpec((FOX_W, tn), lambda i, j: (0, j)),
            pl.BlockSpec((MEM_W, tn), lambda i, j: (0, j)),
        ],
        out_specs=pl.BlockSpec((tm, tn), lambda i, j: (i, j)),
        out_shape=jax.ShapeDtypeStruct((M, D_MODEL), BF),
        scratch_shapes=[pltpu.VMEM((tm, MLA_W), BF), pltpu.VMEM((tm, FOX_W), BF)],
        compiler_params=_params(2),
        name="merge",
    )(oa, of, om, hb, hb, hb, hb, hb, wa, wf, wm)


def _out_ln_kernel(mg_ref, w_ref, x_ref, g_ref, b_ref, o_ref, *, alpha):
    y = _dot(mg_ref[...], w_ref[...])
    z = alpha * x_ref[...] + y
    mu = jnp.mean(z, axis=1, keepdims=True)
    zc = z - mu
    var = jnp.mean(zc * zc, axis=1, keepdims=True)
    o_ref[...] = zc * lax.rsqrt(var + LN_EPS) * g_ref[...] + b_ref[...]


def _out_ln(merged, w_out, x, ln_g, ln_b, alpha):
    M = x.shape[0]
    tm = _pick(M, 256)
    return pl.pallas_call(
        functools.partial(_out_ln_kernel, alpha=alpha),
        grid=(M // tm,),
        in_specs=[
            pl.BlockSpec((tm, D_MODEL), lambda i: (i, 0)),
            pl.BlockSpec((D_MODEL, D_MODEL), lambda i: (0, 0)),
            pl.BlockSpec((tm, D_MODEL), lambda i: (i, 0)),
            pl.BlockSpec((1, D_MODEL), lambda i: (0, 0)),
            pl.BlockSpec((1, D_MODEL), lambda i: (0, 0)),
        ],
        out_specs=pl.BlockSpec((tm, D_MODEL), lambda i: (i, 0)),
        out_shape=jax.ShapeDtypeStruct((M, D_MODEL), F32),
        compiler_params=_params(1),
        name="out_ln",
    )(merged, w_out, x, ln_g.reshape(1, -1), ln_b.reshape(1, -1))


def _rms(x, g):
    return x * lax.rsqrt(jnp.mean(x * x, axis=-1, keepdims=True) + RMS_EPS) * g


def _rope_tables(pos):
    half = ROPE_HD // 2
    inv = ROPE_THETA ** (-jnp.arange(half, dtype=F32) / half)
    ang = pos.astype(F32)[:, None] * inv[None, :]
    cos, sin = jnp.cos(ang), jnp.sin(ang)
    return jnp.concatenate([cos, cos], axis=-1), jnp.concatenate([-sin, sin], axis=-1)


def _swap_halves(w):
    half = ROPE_HD // 2
    return jnp.concatenate([w[..., half:], w[..., :half]], axis=-1)


def kernel(x_prompt, x_sample, mem_prompt, cache_mla_ckv, cache_mla_krope, cache_fox_k, cache_fox_v, cache_fox_logf,
           cache_mem_k, cache_mem_v, page_table, w_in, b_fox_f, q_norm_g, kv_norm_g, w_uq, w_uk, w_uv, w_mem_kv,
           w_br_mla, w_br_fox, w_br_mem, w_out, ln_g, ln_b):
    depth = w_in.shape[0]
    assert depth == 1 and x_prompt.shape[0] == 1
    alpha = (2 * depth) ** 0.25
    T = x_prompt.shape[1]
    B, t_new, _ = x_sample.shape
    Ts = B * t_new
    n_pool, page = cache_mla_ckv.shape[1], cache_mla_ckv.shape[2]
    n_pages = page_table.shape[1]
    past = n_pages * page
    n_mem = mem_prompt.shape[1]

    w = w_in[0]
    o_q, o_c, o_kr, o_ga, o_fq, o_fk, o_fv, o_fl, o_gf, o_mq, o_br = (
        0, 512, 1024, 1088, 3136, 5184, 5440, 5696, 5712, 7760, 8272)
    kr_w = w[:, o_kr:o_kr + ROPE_HD]
    w_a = jnp.concatenate([
        w[:, o_q:o_kr], kr_w, _swap_halves(kr_w), w[:, o_fk:o_fl], w[:, o_fl:o_fl + FOX_HEADS],
        jnp.zeros((D_MODEL, A_END - A_FLOG - FOX_HEADS), F32)], axis=1).astype(BF)
    w_b = jnp.concatenate([w[:, o_ga:o_fq], w[:, o_gf:o_mq], w[:, o_br:], w[:, o_fq:o_fk], w[:, o_mq:o_br]],
                          axis=1).astype(BF)
    uq = w_uq[0].reshape(Q_LORA, MLA_HEADS, NOPE_HD + ROPE_HD)
    uq_r = uq[:, :, NOPE_HD:]
    w_q = jnp.concatenate([uq[:, :, :NOPE_HD].reshape(Q_LORA, -1), uq_r.reshape(Q_LORA, -1),
                           _swap_halves(uq_r).reshape(Q_LORA, -1)], axis=1).astype(BF)
    w_kv = jnp.concatenate([w_uk[0].reshape(KV_LORA, -1), w_uv[0].reshape(KV_LORA, -1)], axis=1).astype(BF)
    w_uk_t = w_uk[0].transpose(1, 2, 0).astype(BF)
    w_uv_h = w_uv[0].transpose(1, 0, 2).astype(BF)

    x_all = jnp.concatenate([x_prompt[0], x_sample.reshape(Ts, D_MODEL)], axis=0)
    ha = _mm(x_all, w_a, F32, 1024, 896, "proj_a")
    hb = _mm(x_all, w_b, BF, 1024, 512, "proj_b")

    pos = jnp.concatenate([jnp.arange(T, dtype=jnp.int32),
                           jnp.tile(past + jnp.arange(t_new, dtype=jnp.int32), B)])
    cos2, sin2 = _rope_tables(pos)
    qn = _rms(ha[:, A_QLAT:A_CKV], q_norm_g[0])
    ckv = _rms(ha[:, A_CKV:A_KR], kv_norm_g[0])
    krope = ha[:, A_KR:A_KRS] * cos2 + ha[:, A_KRS:A_FK] * sin2
    fox_k = ha[:, A_FK:A_FV]
    fox_v = ha[:, A_FV:A_FLOG]
    log_f = jax.nn.log_sigmoid(ha[:, A_FLOG:A_FLOG + FOX_HEADS] + b_fox_f[0])

    hq = _mm(qn, w_q, F32, 1024, 1024, "proj_q")
    q_nope = hq[:, :MLA_W].astype(BF)
    q_rope = (hq[:, MLA_W:MLA_W + MLA_HEADS * ROPE_HD] * jnp.tile(cos2, (1, MLA_HEADS))
              + hq[:, MLA_W + MLA_HEADS * ROPE_HD:] * jnp.tile(sin2, (1, MLA_HEADS))).astype(BF)

    kv = _mm(ckv[:T], w_kv, BF, 1024, 1024, "proj_kv")
    o_a_p = _mla_prompt(q_nope[:T], q_rope[:T], kv[:, :MLA_W], krope[:T].astype(BF), kv[:, MLA_W:])
    fcum = jnp.cumsum(log_f[:T], axis=0)
    o_f_p = _fox_prompt(hb, fox_k[:T].astype(BF), fox_v[:T].astype(BF), fcum, T)
    mkv = _mm(mem_prompt[0], w_mem_kv[0].astype(BF), F32, 256, 512, "proj_mem")
    mq = hb[:, B_MQ:B_END]
    o_m_p = _mem_attend(mq[:T][None], mkv[None, :, :MEM_W], mkv[None, :, MEM_W:], 512, BF, "mem_prompt")[0]

    q_abs = _headwise_mm(q_nope[T:], w_uk_t, BF, 1024, "q_absorb")
    qa = q_abs.reshape(B, t_new * MLA_HEADS, KV_LORA)
    qr = q_rope[T:].reshape(B, t_new * MLA_HEADS, ROPE_HD)
    qf = hb[T:, B_FQ:B_MQ].reshape(B, t_new, FOX_KV_HEADS, FOX_GROUP, FOX_HD).transpose(0, 2, 1, 3, 4)
    qf = qf.reshape(B, FOX_KV_HEADS, t_new * FOX_GROUP, FOX_HD)
    fl_pool = jnp.cumsum(cache_fox_logf[0], axis=1).transpose(0, 2, 1)

    def new_page(a):
        a = a.reshape(B, t_new, a.shape[-1])
        return jnp.pad(a, ((0, 0), (0, page - t_new), (0, 0)))

    fl_new = jnp.cumsum(log_f[T:].reshape(B, t_new, FOX_HEADS), axis=1)
    fl_new = jnp.pad(fl_new, ((0, 0), (0, page - t_new), (0, 0))).transpose(0, 2, 1)
    o_lat, o_f_s = _decode(
        page_table, qa, qr, qf,
        cache_mla_ckv[0], cache_mla_krope[0], cache_fox_k[0].reshape(n_pool, page, FOX_KV_W),
        cache_fox_v[0].reshape(n_pool, page, FOX_KV_W), fl_pool,
        new_page(ckv[T:]).astype(BF), new_page(krope[T:]).astype(BF), new_page(fox_k[T:]).astype(BF),
        new_page(fox_v[T:]).astype(BF), fl_new)
    o_a_s = _headwise_mm(o_lat.reshape(Ts, MLA_HEADS * KV_LORA), w_uv_h, BF, 1024, "v_up")
    o_f_s = o_f_s.reshape(B, FOX_KV_HEADS, t_new, FOX_GROUP, FOX_HD).transpose(0, 2, 1, 3, 4).reshape(Ts, FOX_W)
    o_m_s = _mem_attend(mq[T:].astype(F32).reshape(B, t_new, MEM_W), cache_mem_k[0].reshape(B, n_mem, MEM_W),
                        cache_mem_v[0].reshape(B, n_mem, MEM_W), t_new, F32, "mem_sample")
    o_m_s = o_m_s.reshape(Ts, MEM_W).astype(BF)

    merged = _merge(jnp.concatenate([o_a_p, o_a_s]), jnp.concatenate([o_f_p, o_f_s]),
                    jnp.concatenate([o_m_p, o_m_s]), hb,
                    w_br_mla[0].astype(BF), w_br_fox[0].astype(BF), w_br_mem[0].astype(BF))
    y = _out_ln(merged, w_out[0].astype(BF), x_all, ln_g[0], ln_b[0], alpha)

    mk = mkv[:, :MEM_W].reshape(1, 1, n_mem, MEM_HEADS, MEM_HD)
    mv = mkv[:, MEM_W:].reshape(1, 1, n_mem, MEM_HEADS, MEM_HD)
    return (
        y[:T][None], y[T:].reshape(B, t_new, D_MODEL),
        ckv[:T][None, None], krope[:T][None, None],
        fox_k[:T].reshape(1, 1, T, FOX_KV_HEADS, FOX_HD), fox_v[:T].reshape(1, 1, T, FOX_KV_HEADS, FOX_HD),
        log_f[:T][None, None], mk, mv,
        ckv[T:].reshape(1, B, t_new, KV_LORA), krope[T:].reshape(1, B, t_new, ROPE_HD),
        fox_k[T:].reshape(1, B, t_new, FOX_KV_HEADS, FOX_HD), fox_v[T:].reshape(1, B, t_new, FOX_KV_HEADS, FOX_HD),
        log_f[T:].reshape(1, B, t_new, FOX_HEADS),
    )
```

```python
import functools

import jax
import jax.numpy as jnp
from jax import lax
from jax.experimental import pallas as pl
from jax.experimental.pallas import tpu as pltpu

BF = jnp.bfloat16
F32 = jnp.float32

D_MODEL = 2048
MLA_HEADS = 16
Q_LORA = 512
KV_LORA = 512
NOPE_HD = 128
ROPE_HD = 64
V_HD = 128
MLA_W = MLA_HEADS * V_HD
MLA_SCALE = (NOPE_HD + ROPE_HD) ** -0.5
ROPE_THETA = 10000.0
FOX_HEADS = 16
FOX_KV_HEADS = 2
FOX_HD = 128
FOX_GROUP = FOX_HEADS // FOX_KV_HEADS
FOX_W = FOX_HEADS * FOX_HD
FOX_KV_W = FOX_KV_HEADS * FOX_HD
FOX_SCALE = FOX_HD ** -0.5
MEM_HEADS = 4
MEM_HD = 128
MEM_W = MEM_HEADS * MEM_HD
MEM_SCALE = MEM_HD ** -0.5
N_BRANCH = 3
RMS_EPS = 1e-6
LN_EPS = 1e-5

NEG = -1e30
LOG2E = 1.4426950408889634
VMEM_LIMIT = 56 * 1024 * 1024
PAGES_PER_STEP = 8
LANES = 128
FLASH_TQ, FLASH_TK = 1024, 512

A_QLAT, A_CKV, A_KR, A_KRS, A_FK, A_FV, A_FLOG, A_END = 0, 512, 1024, 1088, 1152, 1408, 1664, 1792
B_GA, B_GF, B_BR, B_FQ, B_MQ, B_END = 0, 2048, 4096, 10240, 12288, 12800


def _pick(n, target, align=8):
    if n <= target:
        return n
    for d in range(target, 0, -1):
        if n % d == 0 and d % align == 0:
            return d
    raise ValueError((n, target, align))


def _params(n_axes):
    return pltpu.CompilerParams(dimension_semantics=("arbitrary",) * n_axes, vmem_limit_bytes=VMEM_LIMIT)


def _dot_nt(a, b):
    return lax.dot_general(a, b, (((1,), (1,)), ((), ())), preferred_element_type=F32)


def _dot(a, b):
    return jnp.dot(a, b, preferred_element_type=F32)


def _mm_kernel(a_ref, b_ref, o_ref, abf_ref):
    @pl.when(pl.program_id(1) == 0)
    def _():
        abf_ref[...] = a_ref[...].astype(BF)

    o_ref[...] = _dot(abf_ref[...], b_ref[...]).astype(o_ref.dtype)


def _mm(a, b, out_dtype, tm, tn, name):
    M, K = a.shape
    N = b.shape[1]
    tm = _pick(M, tm)
    tn = _pick(N, tn, 128)
    return pl.pallas_call(
        _mm_kernel,
        grid=(M // tm, N // tn),
        in_specs=[pl.BlockSpec((tm, K), lambda i, j: (i, 0)), pl.BlockSpec((K, tn), lambda i, j: (0, j))],
        out_specs=pl.BlockSpec((tm, tn), lambda i, j: (i, j)),
        out_shape=jax.ShapeDtypeStruct((M, N), out_dtype),
        scratch_shapes=[pltpu.VMEM((tm, K), BF)],
        compiler_params=_params(2),
        name=name,
    )(a, b)


def _headwise_kernel(a_ref, w_ref, o_ref):
    o_ref[...] = _dot(a_ref[...], w_ref[...]).astype(o_ref.dtype)


def _headwise_mm(a, w, out_dtype, tm, name):
    M = a.shape[0]
    H, K, N = w.shape
    tm = _pick(M, tm)
    return pl.pallas_call(
        _headwise_kernel,
        grid=(H, M // tm),
        in_specs=[pl.BlockSpec((tm, K), lambda h, i: (i, h)), pl.BlockSpec((None, K, N), lambda h, i: (h, 0, 0))],
        out_specs=pl.BlockSpec((tm, N), lambda h, i: (i, h)),
        out_shape=jax.ShapeDtypeStruct((M, H * N), out_dtype),
        compiler_params=_params(2),
        name=name,
    )(a, w)


def _rep(x, n):
    return x if n == LANES else jnp.tile(x, (1, n // LANES))


def _online_softmax(slot, s, v, first=False):
    m_ref, l_ref, acc_ref = slot
    rows, keys = s.shape
    m_blk = jnp.broadcast_to(jnp.max(s, axis=1, keepdims=True), (rows, LANES))
    if first:
        m_new = m_blk
    else:
        m_prev = m_ref[...]
        m_new = jnp.maximum(m_prev, m_blk)
        alpha = jnp.exp2(m_prev - m_new)
    p = jnp.exp2(s - _rep(m_new, keys))
    l_blk = jnp.broadcast_to(jnp.sum(p, axis=1, keepdims=True), (rows, LANES))
    pv = _dot(p.astype(BF), v)
    if first:
        l_ref[...] = l_blk
        acc_ref[...] = pv
    else:
        l_ref[...] = alpha * l_ref[...] + l_blk
        acc_ref[...] = _rep(alpha, acc_ref.shape[-1]) * acc_ref[...] + pv
    m_ref[...] = m_new


def _causal_flash(i, tq, tk, scores, values, slot):
    r = tq // tk
    row = lax.broadcasted_iota(jnp.int32, (tq, tk), 0)
    col = lax.broadcasted_iota(jnp.int32, (tq, tk), 1)
    for d in range(r):
        s = jnp.where(col + d * tk <= row, scores(r * i + d), NEG)
        _online_softmax(slot, s, values(r * i + d), first=(d == 0))

    def body(j, carry):
        _online_softmax(slot, scores(j), values(j))
        return carry

    lax.fori_loop(0, r * i, body, 0)


def _key_rows(j, tk):
    return pl.ds(pl.multiple_of(j * tk, tk), tk)


def _mla_prompt_kernel(qn_ref, qr_ref, kn_ref, kr_ref, v_ref, o_ref, kcat_ref, qcat_ref, m_ref, l_ref, acc_ref, *,
                       tq, tk, T):
    i = pl.program_id(1)
    qk_dim = NOPE_HD + ROPE_HD

    @pl.when(i == 0)
    def _():
        for c in range(T // tk):
            rows = slice(c * tk, (c + 1) * tk)
            for hh in range(2):
                kcat_ref[hh, rows, 0:NOPE_HD] = kn_ref[rows, hh * NOPE_HD:(hh + 1) * NOPE_HD]
                kcat_ref[hh, rows, NOPE_HD:qk_dim] = kr_ref[rows, :]

    for hh in range(2):
        qcat_ref[:, 0:NOPE_HD] = qn_ref[:, hh * NOPE_HD:(hh + 1) * NOPE_HD]
        qcat_ref[:, NOPE_HD:qk_dim] = qr_ref[:, hh * ROPE_HD:(hh + 1) * ROPE_HD]

        def scores(j, hh=hh):
            return _dot_nt(qcat_ref[...], kcat_ref[hh, _key_rows(j, tk), :]) * (MLA_SCALE * LOG2E)

        def values(j, hh=hh):
            return v_ref[_key_rows(j, tk), hh * V_HD:(hh + 1) * V_HD]

        _causal_flash(i, tq, tk, scores, values, (m_ref, l_ref, acc_ref))
        o_ref[:, hh * V_HD:(hh + 1) * V_HD] = (acc_ref[...] / l_ref[...]).astype(o_ref.dtype)


def _flash_tiles(T):
    tk = _pick(T, FLASH_TK)
    tq = _pick(T, FLASH_TQ, tk)
    return tq, tk


def _mla_prompt(qn, qr, kn, kr, v):
    T = qn.shape[0]
    tq, tk = _flash_tiles(T)
    return pl.pallas_call(
        functools.partial(_mla_prompt_kernel, tq=tq, tk=tk, T=T),
        grid=(MLA_HEADS // 2, T // tq),
        in_specs=[
            pl.BlockSpec((tq, 2 * NOPE_HD), lambda h, i: (i, h)),
            pl.BlockSpec((tq, 2 * ROPE_HD), lambda h, i: (i, h)),
            pl.BlockSpec((T, 2 * NOPE_HD), lambda h, i: (0, h)),
            pl.BlockSpec((T, ROPE_HD), lambda h, i: (0, 0)),
            pl.BlockSpec((T, 2 * V_HD), lambda h, i: (0, h)),
        ],
        out_specs=pl.BlockSpec((tq, 2 * V_HD), lambda h, i: (i, h)),
        out_shape=jax.ShapeDtypeStruct((T, MLA_W), BF),
        scratch_shapes=[pltpu.VMEM((2, T, NOPE_HD + ROPE_HD), BF), pltpu.VMEM((tq, NOPE_HD + ROPE_HD), BF),
                        pltpu.VMEM((tq, LANES), F32), pltpu.VMEM((tq, LANES), F32), pltpu.VMEM((tq, V_HD), F32)],
        compiler_params=_params(2),
        name="mla_prompt",
    )(qn, qr, kn, kr, v)


def _fox_prompt_kernel(q_ref, k_ref, v_ref, fq_ref, fk_ref, o_ref, fqb_ref, m_ref, l_ref, acc_ref, *, tq, tk):
    i = pl.program_id(1)
    fq_all = fq_ref[...]
    for h in range(FOX_GROUP):
        sl = slice(h * FOX_HD, (h + 1) * FOX_HD)
        fqb_ref[...] = jnp.broadcast_to(fq_all[:, h:h + 1], (tq, LANES))

        def scores(j, h=h, sl=sl):
            fk = fk_ref[h, pl.ds(j, 1), :]
            return (_dot_nt(q_ref[:, sl], k_ref[_key_rows(j, tk), :]) * (FOX_SCALE * LOG2E)
                    + (_rep(fqb_ref[...], tk) - fk))

        def values(j):
            return v_ref[_key_rows(j, tk), :]

        _causal_flash(i, tq, tk, scores, values, (m_ref, l_ref, acc_ref))
        o_ref[:, sl] = (acc_ref[...] / l_ref[...]).astype(o_ref.dtype)


def _fox_prompt(hb, k, v, fcum, T):
    tq, tk = _flash_tiles(T)
    gw = FOX_GROUP * FOX_HD
    fk3 = fcum.T.reshape(FOX_KV_HEADS, FOX_GROUP, T // tk, tk)
    fq3 = fcum.reshape(T, FOX_KV_HEADS, FOX_GROUP).transpose(1, 0, 2)
    return pl.pallas_call(
        functools.partial(_fox_prompt_kernel, tq=tq, tk=tk),
        grid=(FOX_KV_HEADS, T // tq),
        in_specs=[
            pl.BlockSpec((tq, gw), lambda g, i: (i, B_FQ // gw + g)),
            pl.BlockSpec((T, FOX_HD), lambda g, i: (0, g)),
            pl.BlockSpec((T, FOX_HD), lambda g, i: (0, g)),
            pl.BlockSpec((None, tq, FOX_GROUP), lambda g, i: (g, i, 0)),
            pl.BlockSpec((None, FOX_GROUP, T // tk, tk), lambda g, i: (g, 0, 0, 0)),
        ],
        out_specs=pl.BlockSpec((tq, gw), lambda g, i: (i, g)),
        out_shape=jax.ShapeDtypeStruct((T, FOX_W), BF),
        scratch_shapes=[pltpu.VMEM((tq, LANES), F32), pltpu.VMEM((tq, LANES), F32), pltpu.VMEM((tq, LANES), F32),
                        pltpu.VMEM((tq, FOX_HD), F32)],
        compiler_params=_params(2),
        name="fox_prompt",
    )(hb, k, v, fq3, fk3)


def _mem_kernel(q_ref, k_ref, v_ref, o_ref, *, n_mem, token_major):
    for h in range(MEM_HEADS):
        sl = slice(h * MEM_HD, (h + 1) * MEM_HD)
        q = q_ref[:, sl].astype(BF)
        if token_major:
            k = k_ref[pl.ds(h, n_mem, stride=MEM_HEADS), :].astype(BF)
            v = v_ref[pl.ds(h, n_mem, stride=MEM_HEADS), :].astype(BF)
        else:
            k = k_ref[:, sl].astype(BF)
            v = v_ref[:, sl].astype(BF)
        s = _dot_nt(q, k) * (MEM_SCALE * LOG2E)
        m = jnp.max(s, axis=1, keepdims=True)
        p = jnp.exp2(s - m)
        l = jnp.sum(p, axis=1, keepdims=True)
        o_ref[:, sl] = (_dot(p.astype(BF), v) / l).astype(o_ref.dtype)


def _mem_attend(q, k, v, tq, out_dtype, name):
    B, Tq, _ = q.shape
    token_major = k.shape[2] == MEM_HD
    n_mem = k.shape[1] // MEM_HEADS if token_major else k.shape[1]
    tq = _pick(Tq, tq)
    kv_spec = pl.BlockSpec((None,) + k.shape[1:], lambda b, i: (b, 0, 0))
    return pl.pallas_call(
        functools.partial(_mem_kernel, n_mem=n_mem, token_major=token_major),
        grid=(B, Tq // tq),
        in_specs=[pl.BlockSpec((None, tq, MEM_W), lambda b, i: (b, i, 0)), kv_spec, kv_spec],
        out_specs=pl.BlockSpec((None, tq, MEM_W), lambda b, i: (b, i, 0)),
        out_shape=jax.ShapeDtypeStruct((B, Tq, MEM_W), out_dtype),
        compiler_params=_params(2),
        name=name,
    )(q, k, v)


def _decode_kernel(pt_ref, qa_ref, qr_ref, qf_ref, ckvn_ref, krn_ref, fkn_ref, fvn_ref, lfn_ref,
                   ckv_hbm, kr_hbm, fk_hbm, fv_hbm, lf_hbm, oa_ref, of_ref,
                   ckv_buf, kr_buf, fk_buf, fv_buf, lf_buf, sem, kall, krall, fkall, fvall, bias, fbase,
                   m_a, l_a, acc_a, m_f, l_f, acc_f, *, P, n_steps, n_total, page, t_new):
    step = pl.program_id(0)
    j = step % n_steps
    slot = step % 2
    tk = P * page
    n_rows_a = t_new * MLA_HEADS
    n_rows_f = t_new * FOX_GROUP

    def page_copies(step_, slot_):
        copies = []
        for p in range(P):
            idx = pt_ref[step_ * P + p]
            copies += [
                pltpu.make_async_copy(ckv_hbm.at[idx], ckv_buf.at[slot_, pl.ds(p * page, page), :], sem.at[slot_, 0]),
                pltpu.make_async_copy(kr_hbm.at[idx], kr_buf.at[slot_, :, pl.ds(p * page, page)], sem.at[slot_, 1]),
                pltpu.make_async_copy(fk_hbm.at[idx], fk_buf.at[slot_, pl.ds(p * FOX_KV_HEADS * page, FOX_KV_HEADS * page), :],
                                      sem.at[slot_, 2]),
                pltpu.make_async_copy(fv_hbm.at[idx], fv_buf.at[slot_, pl.ds(p * FOX_KV_HEADS * page, FOX_KV_HEADS * page), :],
                                      sem.at[slot_, 3]),
                pltpu.make_async_copy(lf_hbm.at[idx], lf_buf.at[slot_, pl.ds(p * FOX_HEADS, FOX_HEADS), :],
                                      sem.at[slot_, 4]),
            ]
        return copies

    @pl.when(step == 0)
    def _():
        for c in page_copies(0, 0):
            c.start()

    @pl.when(step + 1 < n_total)
    def _():
        for c in page_copies(step + 1, 1 - slot):
            c.start()

    @pl.when(j == 0)
    def _():
        m_a[...] = jnp.full(m_a.shape, NEG, F32)
        l_a[...] = jnp.zeros(l_a.shape, F32)
        acc_a[...] = jnp.zeros(acc_a.shape, F32)
        m_f[...] = jnp.full(m_f.shape, NEG, F32)
        l_f[...] = jnp.zeros(l_f.shape, F32)
        acc_f[...] = jnp.zeros(acc_f.shape, F32)
        fbase[...] = jnp.zeros(fbase.shape, F32)

    slots = [(m_a, l_a, acc_a)] + [(m_f.at[g], l_f.at[g], acc_f.at[g]) for g in range(FOX_KV_HEADS)]
    tri = (lax.broadcasted_iota(jnp.int32, (page, page), 0) <= lax.broadcasted_iota(jnp.int32, (page, page), 1)
           ).astype(F32)

    def page_cumsum(lf):
        return jnp.dot(lf, tri, precision=lax.Precision.HIGHEST, preferred_element_type=F32)

    def attend(k, kr_t, fk, fv, b, mask_a, mask_f):
        scores = [(_dot_nt(qa_ref[...], k) + _dot(qr_ref[...], kr_t)) * (MLA_SCALE * LOG2E)]
        values = [k]
        for g in range(FOX_KV_HEADS):
            sl = slice(g * FOX_HD, (g + 1) * FOX_HD)
            bg = b[g * FOX_GROUP:(g + 1) * FOX_GROUP, :]
            scores.append(_dot_nt(qf_ref[g], fk[:, sl]) * (FOX_SCALE * LOG2E) - jnp.concatenate([bg] * t_new, axis=0))
            values.append(fv[:, sl])
        if mask_a is not None:
            scores = [jnp.where(mask_a, scores[0], NEG)] + [jnp.where(mask_f, s, NEG) for s in scores[1:]]
        for slot, s, v in zip(slots, scores, values):
            _online_softmax(slot, s, v)

    for c in page_copies(step, slot):
        c.wait()

    kall[...] = ckv_buf[slot].astype(BF)
    krall[...] = kr_buf[slot].astype(BF)
    for g in range(FOX_KV_HEADS):
        sl = slice(g * FOX_HD, (g + 1) * FOX_HD)
        fkall[:, sl] = fk_buf[slot, pl.ds(g, tk, stride=FOX_KV_HEADS), :].astype(BF)
        fvall[:, sl] = fv_buf[slot, pl.ds(g, tk, stride=FOX_KV_HEADS), :].astype(BF)
    fl = page_cumsum(lf_buf[slot])
    base = fbase[...]
    for p in range(P):
        fl_p = fl[p * FOX_HEADS:(p + 1) * FOX_HEADS, :]
        bias[:, p * page:(p + 1) * page] = (base + fl_p) * LOG2E
        base = base + jnp.broadcast_to(fl_p[:, page - 1:page], base.shape)
    fbase[...] = base
    attend(kall[...], krall[...], fkall[...], fvall[...], bias[...], None, None)

    @pl.when(j == n_steps - 1)
    def _():
        col_a = lax.broadcasted_iota(jnp.int32, (n_rows_a, page), 1)
        tok_a = lax.broadcasted_iota(jnp.int32, (n_rows_a, page), 0) // MLA_HEADS
        col_f = lax.broadcasted_iota(jnp.int32, (n_rows_f, page), 1)
        tok_f = lax.broadcasted_iota(jnp.int32, (n_rows_f, page), 0) // FOX_GROUP
        b_new = (fbase[...] + page_cumsum(lfn_ref[...])) * LOG2E
        attend(ckvn_ref[...], krn_ref[...], fkn_ref[...], fvn_ref[...], b_new, col_a <= tok_a, col_f <= tok_f)
        oa_ref[...] = (acc_a[...] / _rep(l_a[...], KV_LORA)).astype(oa_ref.dtype)
        for g in range(FOX_KV_HEADS):
            of_ref[g] = (acc_f[g] / l_f[g]).astype(of_ref.dtype)


def _decode(page_table, qa, qr, qf, ckv_pool, kr_pool, fk_pool, fv_pool, lf_pool, ckv_new, kr_new, fk_new, fv_new,
            lf_new):
    B, n_pages = page_table.shape
    page = ckv_pool.shape[1]
    P = _pick(n_pages, PAGES_PER_STEP, 1)
    n_steps = n_pages // P
    n_total = B * n_steps
    t_new = qa.shape[1] // MLA_HEADS
    n_rows_a = qa.shape[1]
    n_rows_f = qf.shape[2]
    tk = P * page

    def batch_spec(*tail):
        return pl.BlockSpec((None,) + tail, lambda s, pt: (s // n_steps,) + (0,) * len(tail))

    hbm = pl.BlockSpec(memory_space=pl.ANY)
    in_specs = [batch_spec(n_rows_a, KV_LORA), batch_spec(n_rows_a, ROPE_HD), batch_spec(FOX_KV_HEADS, n_rows_f, FOX_HD),
                batch_spec(page, KV_LORA), batch_spec(ROPE_HD, page), batch_spec(page, FOX_KV_W),
                batch_spec(page, FOX_KV_W), batch_spec(FOX_HEADS, page), hbm, hbm, hbm, hbm, hbm]
    scratch = [
        pltpu.VMEM((2, tk, KV_LORA), F32), pltpu.VMEM((2, ROPE_HD, tk), F32),
        pltpu.VMEM((2, FOX_KV_HEADS * tk, FOX_HD), F32), pltpu.VMEM((2, FOX_KV_HEADS * tk, FOX_HD), F32),
        pltpu.VMEM((2, P * FOX_HEADS, page), F32), pltpu.SemaphoreType.DMA((2, 5)),
        pltpu.VMEM((tk, KV_LORA), BF), pltpu.VMEM((ROPE_HD, tk), BF), pltpu.VMEM((tk, FOX_KV_W), BF),
        pltpu.VMEM((tk, FOX_KV_W), BF), pltpu.VMEM((FOX_HEADS, tk), F32), pltpu.VMEM((FOX_HEADS, page), F32),
        pltpu.VMEM((n_rows_a, LANES), F32), pltpu.VMEM((n_rows_a, LANES), F32), pltpu.VMEM((n_rows_a, KV_LORA), F32),
        pltpu.VMEM((FOX_KV_HEADS, n_rows_f, LANES), F32), pltpu.VMEM((FOX_KV_HEADS, n_rows_f, LANES), F32),
        pltpu.VMEM((FOX_KV_HEADS, n_rows_f, FOX_HD), F32),
    ]
    grid_spec = pltpu.PrefetchScalarGridSpec(
        num_scalar_prefetch=1,
        grid=(n_total,),
        in_specs=in_specs,
        out_specs=[batch_spec(n_rows_a, KV_LORA), batch_spec(FOX_KV_HEADS, n_rows_f, FOX_HD)],
        scratch_shapes=scratch,
    )
    return pl.pallas_call(
        functools.partial(_decode_kernel, P=P, n_steps=n_steps, n_total=n_total, page=page, t_new=t_new),
        grid_spec=grid_spec,
        out_shape=[jax.ShapeDtypeStruct((B, n_rows_a, KV_LORA), BF),
                   jax.ShapeDtypeStruct((B, FOX_KV_HEADS, n_rows_f, FOX_HD), BF)],
        compiler_params=_params(1),
        name="decode",
    )(page_table.reshape(-1), qa, qr, qf, ckv_new, kr_new, fk_new, fv_new, lf_new,
      ckv_pool, kr_pool, fk_pool, fv_pool, lf_pool)


def _merge_kernel(oa_ref, of_ref, om_ref, ga_ref, gf_ref, b0_ref, b1_ref, b2_ref, wa_ref, wf_ref, wm_ref, o_ref,
                  acta_ref, actf_ref):
    @pl.when(pl.program_id(1) == 0)
    def _():
        ga = ga_ref[...].astype(F32)
        gf = gf_ref[...].astype(F32)
        acta_ref[...] = (oa_ref[...].astype(F32) * (ga * jax.nn.sigmoid(ga))).astype(BF)
        actf_ref[...] = (of_ref[...].astype(F32) * (gf * jax.nn.sigmoid(gf))).astype(BF)

    a = _dot(acta_ref[...], wa_ref[...])
    f = _dot(actf_ref[...], wf_ref[...])
    m = _dot(om_ref[...], wm_ref[...])
    merged = (jax.nn.sigmoid(b0_ref[...].astype(F32)) * a + jax.nn.sigmoid(b1_ref[...].astype(F32)) * f
              + jax.nn.sigmoid(b2_ref[...].astype(F32)) * m)
    o_ref[...] = merged.astype(o_ref.dtype)


def _merge(oa, of, om, hb, wa, wf, wm):
    M = oa.shape[0]
    tm = _pick(M, 512)
    tn = 512
    nb = D_MODEL // tn
    row = lambda c: (lambda i, j: (i, c))
    return pl.pallas_call(
        _merge_kernel,
        grid=(M // tm, nb),
        in_specs=[
            pl.BlockSpec((tm, MLA_W), row(0)),
            pl.BlockSpec((tm, FOX_W), row(0)),
            pl.BlockSpec((tm, MEM_W), row(0)),
            pl.BlockSpec((tm, MLA_W), row(B_GA // MLA_W)),
            pl.BlockSpec((tm, FOX_W), row(B_GF // FOX_W)),
            pl.BlockSpec((tm, tn), lambda i, j: (i, B_BR // tn + j)),
            pl.BlockSpec((tm, tn), lambda i, j: (i, B_BR // tn + nb + j)),
            pl.BlockSpec((tm, tn), lambda i, j: (i, B_BR // tn + 2 * nb + j)),
            pl.BlockSpec((MLA_W, tn), lambda i, j: (0, j)),
            pl.BlockSpec((FOX_W, tn), lambda i, j: (0, j)),
            pl.BlockSpec((MEM_W, tn), lambda i, j: (0, j)),
        ],
        out_specs=pl.BlockSpec((tm, tn), lambda i, j: (i, j)),
        out_shape=jax.ShapeDtypeStruct((M, D_MODEL), BF),
        scratch_shapes=[pltpu.VMEM((tm, MLA_W), BF), pltpu.VMEM((tm, FOX_W), BF)],
        compiler_params=_params(2),
        name="merge",
    )(oa, of, om, hb, hb, hb, hb, hb, wa, wf, wm)


def _out_ln_kernel(mg_ref, w_ref, x_ref, g_ref, b_ref, o_ref, *, alpha):
    y = _dot(mg_ref[...], w_ref[...])
    z = alpha * x_ref[...] + y
    mu = jnp.mean(z, axis=1, keepdims=True)
    zc = z - mu
    var = jnp.mean(zc * zc, axis=1, keepdims=True)
    o_ref[...] = zc * lax.rsqrt(var + LN_EPS) * g_ref[...] + b_ref[...]


def _out_ln(merged, w_out, x, ln_g, ln_b, alpha):
    M = x.shape[0]
    tm = _pick(M, 256)
    return pl.pallas_call(
        functools.partial(_out_ln_kernel, alpha=alpha),
        grid=(M // tm,),
        in_specs=[
            pl.BlockSpec((tm, D_MODEL), lambda i: (i, 0)),
            pl.BlockSpec((D_MODEL, D_MODEL), lambda i: (0, 0)),
            pl.BlockSpec((tm, D_MODEL), lambda i: (i, 0)),
            pl.BlockSpec((1, D_MODEL), lambda i: (0, 0)),
            pl.BlockSpec((1, D_MODEL), lambda i: (0, 0)),
        ],
        out_specs=pl.BlockSpec((tm, D_MODEL), lambda i: (i, 0)),
        out_shape=jax.ShapeDtypeStruct((M, D_MODEL), F32),
        compiler_params=_params(1),
        name="out_ln",
    )(merged, w_out, x, ln_g.reshape(1, -1), ln_b.reshape(1, -1))


def _rms(x, g):
    return x * lax.rsqrt(jnp.mean(x * x, axis=-1, keepdims=True) + RMS_EPS) * g


def _rope_tables(pos):
    half = ROPE_HD // 2
    inv = ROPE_THETA ** (-jnp.arange(half, dtype=F32) / half)
    ang = pos.astype(F32)[:, None] * inv[None, :]
    cos, sin = jnp.cos(ang), jnp.sin(ang)
    return jnp.concatenate([cos, cos], axis=-1), jnp.concatenate([-sin, sin], axis=-1)


def _swap_halves(w):
    half = ROPE_HD // 2
    return jnp.concatenate([w[..., half:], w[..., :half]], axis=-1)


def kernel(x_prompt, x_sample, mem_prompt, cache_mla_ckv, cache_mla_krope, cache_fox_k, cache_fox_v, cache_fox_logf,
           cache_mem_k, cache_mem_v, page_table, w_in, b_fox_f, q_norm_g, kv_norm_g, w_uq, w_uk, w_uv, w_mem_kv,
           w_br_mla, w_br_fox, w_br_mem, w_out, ln_g, ln_b):
    depth = w_in.shape[0]
    assert depth == 1 and x_prompt.shape[0] == 1
    alpha = (2 * depth) ** 0.25
    T = x_prompt.shape[1]
    B, t_new, _ = x_sample.shape
    Ts = B * t_new
    n_pool, page = cache_mla_ckv.shape[1], cache_mla_ckv.shape[2]
    n_pages = page_table.shape[1]
    past = n_pages * page
    n_mem = mem_prompt.shape[1]

    w = w_in[0]
    o_q, o_c, o_kr, o_ga, o_fq, o_fk, o_fv, o_fl, o_gf, o_mq, o_br = (
        0, 512, 1024, 1088, 3136, 5184, 5440, 5696, 5712, 7760, 8272)
    kr_w = w[:, o_kr:o_kr + ROPE_HD]
    w_a = jnp.concatenate([
        w[:, o_q:o_kr], kr_w, _swap_halves(kr_w), w[:, o_fk:o_fl], w[:, o_fl:o_fl + FOX_HEADS],
        jnp.zeros((D_MODEL, A_END - A_FLOG - FOX_HEADS), F32)], axis=1).astype(BF)
    w_b = jnp.concatenate([w[:, o_ga:o_fq], w[:, o_gf:o_mq], w[:, o_br:], w[:, o_fq:o_fk], w[:, o_mq:o_br]],
                          axis=1).astype(BF)
    uq = w_uq[0].reshape(Q_LORA, MLA_HEADS, NOPE_HD + ROPE_HD)
    uq_r = uq[:, :, NOPE_HD:]
    w_q = jnp.concatenate([uq[:, :, :NOPE_HD].reshape(Q_LORA, -1), uq_r.reshape(Q_LORA, -1),
                           _swap_halves(uq_r).reshape(Q_LORA, -1)], axis=1).astype(BF)
    w_kv = jnp.concatenate([w_uk[0].reshape(KV_LORA, -1), w_uv[0].reshape(KV_LORA, -1)], axis=1).astype(BF)
    w_uk_t = w_uk[0].transpose(1, 2, 0).astype(BF)
    w_uv_h = w_uv[0].transpose(1, 0, 2).astype(BF)

    x_all = jnp.concatenate([x_prompt[0], x_sample.reshape(Ts, D_MODEL)], axis=0)
    ha = _mm(x_all, w_a, F32, 1024, 896, "proj_a")
    hb = _mm(x_all, w_b, BF, 1024, 512, "proj_b")

    pos = jnp.concatenate([jnp.arange(T, dtype=jnp.int32),
                           jnp.tile(past + jnp.arange(t_new, dtype=jnp.int32), B)])
    cos2, sin2 = _rope_tables(pos)
    qn = _rms(ha[:, A_QLAT:A_CKV], q_norm_g[0])
    ckv = _rms(ha[:, A_CKV:A_KR], kv_norm_g[0])
    krope = ha[:, A_KR:A_KRS] * cos2 + ha[:, A_KRS:A_FK] * sin2
    fox_k = ha[:, A_FK:A_FV]
    fox_v = ha[:, A_FV:A_FLOG]
    log_f = jax.nn.log_sigmoid(ha[:, A_FLOG:A_FLOG + FOX_HEADS] + b_fox_f[0])

    hq = _mm(qn, w_q, F32, 1024, 1024, "proj_q")
    q_nope = hq[:, :MLA_W].astype(BF)
    q_rope = (hq[:, MLA_W:MLA_W + MLA_HEADS * ROPE_HD] * jnp.tile(cos2, (1, MLA_HEADS))
              + hq[:, MLA_W + MLA_HEADS * ROPE_HD:] * jnp.tile(sin2, (1, MLA_HEADS))).astype(BF)

    kv = _mm(ckv[:T], w_kv, BF, 1024, 1024, "proj_kv")
    o_a_p = _mla_prompt(q_nope[:T], q_rope[:T], kv[:, :MLA_W], krope[:T].astype(BF), kv[:, MLA_W:])
    fcum = jnp.cumsum(log_f[:T], axis=0) * LOG2E
    o_f_p = _fox_prompt(hb, fox_k[:T].astype(BF), fox_v[:T].astype(BF), fcum, T)
    mkv = _mm(mem_prompt[0], w_mem_kv[0].astype(BF), F32, 256, 512, "proj_mem")
    mq = hb[:, B_MQ:B_END]
    o_m_p = _mem_attend(mq[:T][None], mkv[None, :, :MEM_W], mkv[None, :, MEM_W:], 512, BF, "mem_prompt")[0]

    q_abs = _headwise_mm(q_nope[T:], w_uk_t, BF, 1024, "q_absorb")
    qa = q_abs.reshape(B, t_new * MLA_HEADS, KV_LORA)
    qr = q_rope[T:].reshape(B, t_new * MLA_HEADS, ROPE_HD)
    qf = hb[T:, B_FQ:B_MQ].reshape(B, t_new, FOX_KV_HEADS, FOX_GROUP, FOX_HD).transpose(0, 2, 1, 3, 4)
    qf = qf.reshape(B, FOX_KV_HEADS, t_new * FOX_GROUP, FOX_HD)
    def new_page(a):
        a = a.reshape(B, t_new, a.shape[-1])
        return jnp.pad(a, ((0, 0), (0, page - t_new), (0, 0)))

    o_lat, o_f_s = _decode(
        page_table, qa, qr, qf,
        cache_mla_ckv[0], cache_mla_krope[0].transpose(0, 2, 1),
        cache_fox_k[0].reshape(n_pool, page * FOX_KV_HEADS, FOX_HD),
        cache_fox_v[0].reshape(n_pool, page * FOX_KV_HEADS, FOX_HD), cache_fox_logf[0].transpose(0, 2, 1),
        new_page(ckv[T:]).astype(BF), new_page(krope[T:]).astype(BF).transpose(0, 2, 1),
        new_page(fox_k[T:]).astype(BF), new_page(fox_v[T:]).astype(BF), new_page(log_f[T:]).transpose(0, 2, 1))
    o_a_s = _headwise_mm(o_lat.reshape(Ts, MLA_HEADS * KV_LORA), w_uv_h, BF, 1024, "v_up")
    o_f_s = o_f_s.reshape(B, FOX_KV_HEADS, t_new, FOX_GROUP, FOX_HD).transpose(0, 2, 1, 3, 4).reshape(Ts, FOX_W)
    o_m_s = _mem_attend(mq[T:].astype(F32).reshape(B, t_new, MEM_W),
                        cache_mem_k[0].reshape(B, n_mem * MEM_HEADS, MEM_HD),
                        cache_mem_v[0].reshape(B, n_mem * MEM_HEADS, MEM_HD), t_new, F32, "mem_sample")
    o_m_s = o_m_s.reshape(Ts, MEM_W).astype(BF)

    merged = _merge(jnp.concatenate([o_a_p, o_a_s]), jnp.concatenate([o_f_p, o_f_s]),
                    jnp.concatenate([o_m_p, o_m_s]), hb,
                    w_br_mla[0].astype(BF), w_br_fox[0].astype(BF), w_br_mem[0].astype(BF))
    y = _out_ln(merged, w_out[0].astype(BF), x_all, ln_g[0], ln_b[0], alpha)

    mk = mkv[:, :MEM_W].reshape(1, 1, n_mem, MEM_HEADS, MEM_HD)
    mv = mkv[:, MEM_W:].reshape(1, 1, n_mem, MEM_HEADS, MEM_HD)
    return (
        y[:T][None], y[T:].reshape(B, t_new, D_MODEL),
        ckv[:T][None, None], krope[:T][None, None],
        fox_k[:T].reshape(1, 1, T, FOX_KV_HEADS, FOX_HD), fox_v[:T].reshape(1, 1, T, FOX_KV_HEADS, FOX_HD),
        log_f[:T][None, None], mk, mv,
        ckv[T:].reshape(1, B, t_new, KV_LORA), krope[T:].reshape(1, B, t_new, ROPE_HD),
        fox_k[T:].reshape(1, B, t_new, FOX_KV_HEADS, FOX_HD), fox_v[T:].reshape(1, B, t_new, FOX_KV_HEADS, FOX_HD),
        log_f[T:].reshape(1, B, t_new, FOX_HEADS),
    )
```

```python
import functools

import jax
import jax.numpy as jnp
from jax import lax
from jax.experimental import pallas as pl
from jax.experimental.pallas import tpu as pltpu

BF = jnp.bfloat16
F32 = jnp.float32

D_MODEL = 2048
MLA_HEADS = 16
Q_LORA = 512
KV_LORA = 512
NOPE_HD = 128
ROPE_HD = 64
V_HD = 128
MLA_W = MLA_HEADS * V_HD
MLA_SCALE = (NOPE_HD + ROPE_HD) ** -0.5
ROPE_THETA = 10000.0
FOX_HEADS = 16
FOX_KV_HEADS = 2
FOX_HD = 128
FOX_GROUP = FOX_HEADS // FOX_KV_HEADS
FOX_W = FOX_HEADS * FOX_HD
FOX_KV_W = FOX_KV_HEADS * FOX_HD
FOX_SCALE = FOX_HD ** -0.5
MEM_HEADS = 4
MEM_HD = 128
MEM_W = MEM_HEADS * MEM_HD
MEM_SCALE = MEM_HD ** -0.5
N_BRANCH = 3
RMS_EPS = 1e-6
LN_EPS = 1e-5

NEG = -1e30
LOG2E = 1.4426950408889634
VMEM_LIMIT = 56 * 1024 * 1024
PAGES_PER_STEP = 16
LANES = 128
FLASH_TQ, FLASH_TK = 1024, 512

A_QLAT, A_CKV, A_KR, A_KRS, A_FK, A_FV, A_FLOG, A_END = 0, 512, 1024, 1088, 1152, 1408, 1664, 1792
B_GA, B_GF, B_BR, B_FQ, B_MQ, B_END = 0, 2048, 4096, 10240, 12288, 12800


def _pick(n, target, align=8):
    if n <= target:
        return n
    for d in range(target, 0, -1):
        if n % d == 0 and d % align == 0:
            return d
    raise ValueError((n, target, align))


def _params(n_axes):
    return pltpu.CompilerParams(dimension_semantics=("arbitrary",) * n_axes, vmem_limit_bytes=VMEM_LIMIT)


def _dot_nt(a, b):
    return lax.dot_general(a, b, (((1,), (1,)), ((), ())), preferred_element_type=F32)


def _dot(a, b):
    return jnp.dot(a, b, preferred_element_type=F32)


def _mm_kernel(a_ref, b_ref, o_ref, abf_ref):
    @pl.when(pl.program_id(1) == 0)
    def _():
        abf_ref[...] = a_ref[...].astype(BF)

    o_ref[...] = _dot(abf_ref[...], b_ref[...]).astype(o_ref.dtype)


def _mm(a, b, out_dtype, tm, tn, name):
    M, K = a.shape
    N = b.shape[1]
    tm = _pick(M, tm)
    tn = _pick(N, tn, 128)
    return pl.pallas_call(
        _mm_kernel,
        grid=(M // tm, N // tn),
        in_specs=[pl.BlockSpec((tm, K), lambda i, j: (i, 0)), pl.BlockSpec((K, tn), lambda i, j: (0, j))],
        out_specs=pl.BlockSpec((tm, tn), lambda i, j: (i, j)),
        out_shape=jax.ShapeDtypeStruct((M, N), out_dtype),
        scratch_shapes=[pltpu.VMEM((tm, K), BF)],
        compiler_params=_params(2),
        name=name,
    )(a, b)


def _headwise_kernel(a_ref, w_ref, o_ref):
    o_ref[...] = _dot(a_ref[...], w_ref[...]).astype(o_ref.dtype)


def _headwise_mm(a, w, out_dtype, tm, name):
    M = a.shape[0]
    H, K, N = w.shape
    tm = _pick(M, tm)
    return pl.pallas_call(
        _headwise_kernel,
        grid=(H, M // tm),
        in_specs=[pl.BlockSpec((tm, K), lambda h, i: (i, h)), pl.BlockSpec((None, K, N), lambda h, i: (h, 0, 0))],
        out_specs=pl.BlockSpec((tm, N), lambda h, i: (i, h)),
        out_shape=jax.ShapeDtypeStruct((M, H * N), out_dtype),
        compiler_params=_params(2),
        name=name,
    )(a, w)


def _rep(x, n):
    return x if n == LANES else jnp.tile(x, (1, n // LANES))


def _online_softmax(slot, s, v, first=False):
    m_ref, l_ref, acc_ref = slot
    rows, keys = s.shape
    m_blk = jnp.broadcast_to(jnp.max(s, axis=1, keepdims=True), (rows, LANES))
    if first:
        m_new = m_blk
    else:
        m_prev = m_ref[...]
        m_new = jnp.maximum(m_prev, m_blk)
        alpha = jnp.exp2(m_prev - m_new)
    p = jnp.exp2(s - _rep(m_new, keys))
    l_blk = jnp.broadcast_to(jnp.sum(p, axis=1, keepdims=True), (rows, LANES))
    pv = _dot(p.astype(BF), v)
    if first:
        l_ref[...] = l_blk
        acc_ref[...] = pv
    else:
        l_ref[...] = alpha * l_ref[...] + l_blk
        acc_ref[...] = _rep(alpha, acc_ref.shape[-1]) * acc_ref[...] + pv
    m_ref[...] = m_new


def _causal_flash(i, tq, tk, scores, values, slot):
    r = tq // tk
    row = lax.broadcasted_iota(jnp.int32, (tq, tk), 0)
    col = lax.broadcasted_iota(jnp.int32, (tq, tk), 1)
    for d in range(r):
        s = jnp.where(col + d * tk <= row, scores(r * i + d), NEG)
        _online_softmax(slot, s, values(r * i + d), first=(d == 0))

    def body(j, carry):
        _online_softmax(slot, scores(j), values(j))
        return carry

    lax.fori_loop(0, r * i, body, 0)


def _key_rows(j, tk):
    return pl.ds(pl.multiple_of(j * tk, tk), tk)


def _mla_prompt_kernel(qn_ref, qr_ref, kn_ref, kr_ref, v_ref, o_ref, kcat_ref, qcat_ref, m_ref, l_ref, acc_ref, *,
                       tq, tk, T):
    i = pl.program_id(1)
    qk_dim = NOPE_HD + ROPE_HD

    @pl.when(i == 0)
    def _():
        for c in range(T // tk):
            rows = slice(c * tk, (c + 1) * tk)
            for hh in range(2):
                kcat_ref[hh, rows, 0:NOPE_HD] = kn_ref[rows, hh * NOPE_HD:(hh + 1) * NOPE_HD]
                kcat_ref[hh, rows, NOPE_HD:qk_dim] = kr_ref[rows, :]

    for hh in range(2):
        qcat_ref[:, 0:NOPE_HD] = qn_ref[:, hh * NOPE_HD:(hh + 1) * NOPE_HD]
        qcat_ref[:, NOPE_HD:qk_dim] = qr_ref[:, hh * ROPE_HD:(hh + 1) * ROPE_HD]

        def scores(j, hh=hh):
            return _dot_nt(qcat_ref[...], kcat_ref[hh, _key_rows(j, tk), :]) * (MLA_SCALE * LOG2E)

        def values(j, hh=hh):
            return v_ref[_key_rows(j, tk), hh * V_HD:(hh + 1) * V_HD]

        _causal_flash(i, tq, tk, scores, values, (m_ref, l_ref, acc_ref))
        o_ref[:, hh * V_HD:(hh + 1) * V_HD] = (acc_ref[...] / l_ref[...]).astype(o_ref.dtype)


def _flash_tiles(T):
    tk = _pick(T, FLASH_TK)
    tq = _pick(T, FLASH_TQ, tk)
    return tq, tk


def _mla_prompt(qn, qr, kv, kr, T):
    tq, tk = _flash_tiles(T)
    n_kcols = MLA_W // (2 * NOPE_HD)
    return pl.pallas_call(
        functools.partial(_mla_prompt_kernel, tq=tq, tk=tk, T=T),
        grid=(MLA_HEADS // 2, T // tq),
        in_specs=[
            pl.BlockSpec((tq, 2 * NOPE_HD), lambda h, i: (i, h)),
            pl.BlockSpec((tq, 2 * ROPE_HD), lambda h, i: (i, h)),
            pl.BlockSpec((T, 2 * NOPE_HD), lambda h, i: (0, h)),
            pl.BlockSpec((T, ROPE_HD), lambda h, i: (0, 0)),
            pl.BlockSpec((T, 2 * V_HD), lambda h, i: (0, n_kcols + h)),
        ],
        out_specs=pl.BlockSpec((tq, 2 * V_HD), lambda h, i: (i, h)),
        out_shape=jax.ShapeDtypeStruct((T, MLA_W), BF),
        scratch_shapes=[pltpu.VMEM((2, T, NOPE_HD + ROPE_HD), BF), pltpu.VMEM((tq, NOPE_HD + ROPE_HD), BF),
                        pltpu.VMEM((tq, LANES), F32), pltpu.VMEM((tq, LANES), F32), pltpu.VMEM((tq, V_HD), F32)],
        compiler_params=_params(2),
        name="mla_prompt",
    )(qn, qr, kv, kr, kv)


def _fox_prompt_kernel(q_ref, k_ref, v_ref, fq_ref, fk_ref, o_ref, fqb_ref, m_ref, l_ref, acc_ref, *, tq, tk):
    i = pl.program_id(1)
    fq_all = fq_ref[...]
    for h in range(FOX_GROUP):
        sl = slice(h * FOX_HD, (h + 1) * FOX_HD)
        fqb_ref[...] = jnp.broadcast_to(fq_all[:, h:h + 1], (tq, LANES))

        def scores(j, h=h, sl=sl):
            fk = fk_ref[h, pl.ds(j, 1), :]
            return (_dot_nt(q_ref[:, sl], k_ref[_key_rows(j, tk), :]) * (FOX_SCALE * LOG2E)
                    + (_rep(fqb_ref[...], tk) - fk))

        def values(j):
            return v_ref[_key_rows(j, tk), :]

        _causal_flash(i, tq, tk, scores, values, (m_ref, l_ref, acc_ref))
        o_ref[:, sl] = (acc_ref[...] / l_ref[...]).astype(o_ref.dtype)


def _fox_prompt(hb, k, v, fcum, T):
    tq, tk = _flash_tiles(T)
    gw = FOX_GROUP * FOX_HD
    fk3 = fcum.T.reshape(FOX_KV_HEADS, FOX_GROUP, T // tk, tk)
    fq3 = fcum.reshape(T, FOX_KV_HEADS, FOX_GROUP).transpose(1, 0, 2)
    return pl.pallas_call(
        functools.partial(_fox_prompt_kernel, tq=tq, tk=tk),
        grid=(FOX_KV_HEADS, T // tq),
        in_specs=[
            pl.BlockSpec((tq, gw), lambda g, i: (i, B_FQ // gw + g)),
            pl.BlockSpec((T, FOX_HD), lambda g, i: (0, g)),
            pl.BlockSpec((T, FOX_HD), lambda g, i: (0, g)),
            pl.BlockSpec((None, tq, FOX_GROUP), lambda g, i: (g, i, 0)),
            pl.BlockSpec((None, FOX_GROUP, T // tk, tk), lambda g, i: (g, 0, 0, 0)),
        ],
        out_specs=pl.BlockSpec((tq, gw), lambda g, i: (i, g)),
        out_shape=jax.ShapeDtypeStruct((T, FOX_W), BF),
        scratch_shapes=[pltpu.VMEM((tq, LANES), F32), pltpu.VMEM((tq, LANES), F32), pltpu.VMEM((tq, LANES), F32),
                        pltpu.VMEM((tq, FOX_HD), F32)],
        compiler_params=_params(2),
        name="fox_prompt",
    )(hb, k, v, fq3, fk3)


def _mem_kernel(q_ref, k_ref, v_ref, o_ref, *, n_mem, token_major):
    for h in range(MEM_HEADS):
        sl = slice(h * MEM_HD, (h + 1) * MEM_HD)
        q = q_ref[:, sl].astype(BF)
        if token_major:
            k = k_ref[pl.ds(h, n_mem, stride=MEM_HEADS), :].astype(BF)
            v = v_ref[pl.ds(h, n_mem, stride=MEM_HEADS), :].astype(BF)
        else:
            k = k_ref[:, sl].astype(BF)
            v = v_ref[:, sl].astype(BF)
        s = _dot_nt(q, k) * (MEM_SCALE * LOG2E)
        m = jnp.max(s, axis=1, keepdims=True)
        p = jnp.exp2(s - m)
        l = jnp.sum(p, axis=1, keepdims=True)
        o_ref[:, sl] = (_dot(p.astype(BF), v) / l).astype(o_ref.dtype)


def _mem_attend(q, k, v, tq, out_dtype, name):
    B, Tq, _ = q.shape
    token_major = k.shape[2] == MEM_HD
    n_mem = k.shape[1] // MEM_HEADS if token_major else k.shape[1]
    tq = _pick(Tq, tq)
    kv_spec = pl.BlockSpec((None,) + k.shape[1:], lambda b, i: (b, 0, 0))
    return pl.pallas_call(
        functools.partial(_mem_kernel, n_mem=n_mem, token_major=token_major),
        grid=(B, Tq // tq),
        in_specs=[pl.BlockSpec((None, tq, MEM_W), lambda b, i: (b, i, 0)), kv_spec, kv_spec],
        out_specs=pl.BlockSpec((None, tq, MEM_W), lambda b, i: (b, i, 0)),
        out_shape=jax.ShapeDtypeStruct((B, Tq, MEM_W), out_dtype),
        compiler_params=_params(2),
        name=name,
    )(q, k, v)


def _decode_kernel(pt_ref, qa_ref, qr_ref, qf_ref, ckvn_ref, krn_ref, fkn_ref, fvn_ref, lfn_ref,
                   ckv_hbm, kr_hbm, fk_hbm, fv_hbm, lf_hbm, oa_ref, of_ref,
                   ckv_buf, kr_buf, fk_buf, fv_buf, lf_buf, sem, kall, krall, fkall, fvall, bias, fbase,
                   m_a, l_a, acc_a, m_f, l_f, acc_f, *, P, n_steps, n_total, page, t_new):
    step = pl.program_id(0)
    j = step % n_steps
    slot = step % 2
    tk = P * page
    n_rows_a = t_new * MLA_HEADS
    n_rows_f = t_new * FOX_GROUP

    def page_copies(step_, slot_):
        copies = []
        for p in range(P):
            idx = pt_ref[step_ * P + p]
            copies += [
                pltpu.make_async_copy(ckv_hbm.at[idx], ckv_buf.at[slot_, pl.ds(p * page, page), :], sem.at[slot_, 0]),
                pltpu.make_async_copy(kr_hbm.at[idx], kr_buf.at[slot_, :, pl.ds(p * page, page)], sem.at[slot_, 1]),
                pltpu.make_async_copy(fk_hbm.at[idx], fk_buf.at[slot_, pl.ds(p * FOX_KV_HEADS * page, FOX_KV_HEADS * page), :],
                                      sem.at[slot_, 2]),
                pltpu.make_async_copy(fv_hbm.at[idx], fv_buf.at[slot_, pl.ds(p * FOX_KV_HEADS * page, FOX_KV_HEADS * page), :],
                                      sem.at[slot_, 3]),
                pltpu.make_async_copy(lf_hbm.at[idx], lf_buf.at[slot_, pl.ds(p * FOX_HEADS, FOX_HEADS), :],
                                      sem.at[slot_, 4]),
            ]
        return copies

    @pl.when(step == 0)
    def _():
        for c in page_copies(0, 0):
            c.start()

    @pl.when(step + 1 < n_total)
    def _():
        for c in page_copies(step + 1, 1 - slot):
            c.start()

    @pl.when(j == 0)
    def _():
        m_a[...] = jnp.full(m_a.shape, NEG, F32)
        l_a[...] = jnp.zeros(l_a.shape, F32)
        acc_a[...] = jnp.zeros(acc_a.shape, F32)
        m_f[...] = jnp.full(m_f.shape, NEG, F32)
        l_f[...] = jnp.zeros(l_f.shape, F32)
        acc_f[...] = jnp.zeros(acc_f.shape, F32)
        fbase[...] = jnp.zeros(fbase.shape, F32)

    slots = [(m_a, l_a, acc_a)] + [(m_f.at[g], l_f.at[g], acc_f.at[g]) for g in range(FOX_KV_HEADS)]
    tri = (lax.broadcasted_iota(jnp.int32, (page, page), 0) <= lax.broadcasted_iota(jnp.int32, (page, page), 1)
           ).astype(F32)

    def page_cumsum(lf):
        return jnp.dot(lf, tri, precision=lax.Precision.HIGHEST, preferred_element_type=F32)

    def attend(k, kr_t, fk, fv, b, mask_a, mask_f):
        scores = [(_dot_nt(qa_ref[...], k) + _dot(qr_ref[...], kr_t)) * (MLA_SCALE * LOG2E)]
        values = [k]
        for g in range(FOX_KV_HEADS):
            sl = slice(g * FOX_HD, (g + 1) * FOX_HD)
            bg = b[g * FOX_GROUP:(g + 1) * FOX_GROUP, :]
            scores.append(_dot_nt(qf_ref[g], fk[:, sl]) * (FOX_SCALE * LOG2E) - jnp.concatenate([bg] * t_new, axis=0))
            values.append(fv[:, sl])
        if mask_a is not None:
            scores = [jnp.where(mask_a, scores[0], NEG)] + [jnp.where(mask_f, s, NEG) for s in scores[1:]]
        for slot, s, v in zip(slots, scores, values):
            _online_softmax(slot, s, v)

    for c in page_copies(step, slot):
        c.wait()

    kall[...] = ckv_buf[slot].astype(BF)
    krall[...] = kr_buf[slot].astype(BF)
    for g in range(FOX_KV_HEADS):
        sl = slice(g * FOX_HD, (g + 1) * FOX_HD)
        fkall[:, sl] = fk_buf[slot, pl.ds(g, tk, stride=FOX_KV_HEADS), :].astype(BF)
        fvall[:, sl] = fv_buf[slot, pl.ds(g, tk, stride=FOX_KV_HEADS), :].astype(BF)
    fl = page_cumsum(lf_buf[slot])
    base = fbase[...]
    for p in range(P):
        fl_p = fl[p * FOX_HEADS:(p + 1) * FOX_HEADS, :]
        bias[:, p * page:(p + 1) * page] = (base + fl_p) * LOG2E
        base = base + jnp.broadcast_to(fl_p[:, page - 1:page], base.shape)
    fbase[...] = base
    attend(kall[...], krall[...], fkall[...], fvall[...], bias[...], None, None)

    @pl.when(j == n_steps - 1)
    def _():
        col_a = lax.broadcasted_iota(jnp.int32, (n_rows_a, page), 1)
        tok_a = lax.broadcasted_iota(jnp.int32, (n_rows_a, page), 0) // MLA_HEADS
        col_f = lax.broadcasted_iota(jnp.int32, (n_rows_f, page), 1)
        tok_f = lax.broadcasted_iota(jnp.int32, (n_rows_f, page), 0) // FOX_GROUP
        b_new = (fbase[...] + page_cumsum(lfn_ref[...])) * LOG2E
        attend(ckvn_ref[...], krn_ref[...], fkn_ref[...], fvn_ref[...], b_new, col_a <= tok_a, col_f <= tok_f)
        oa_ref[...] = (acc_a[...] / _rep(l_a[...], KV_LORA)).astype(oa_ref.dtype)
        for g in range(FOX_KV_HEADS):
            of_ref[g] = (acc_f[g] / l_f[g]).astype(of_ref.dtype)


def _decode(page_table, qa, qr, qf, ckv_pool, kr_pool, fk_pool, fv_pool, lf_pool, ckv_new, kr_new, fk_new, fv_new,
            lf_new):
    B, n_pages = page_table.shape
    page = ckv_pool.shape[1]
    P = _pick(n_pages, PAGES_PER_STEP, 1)
    n_steps = n_pages // P
    n_total = B * n_steps
    n_rows_a = qa.shape[1]
    t_new = n_rows_a // MLA_HEADS
    n_rows_f = qf.shape[2]
    tk = P * page

    def batch_spec(*tail):
        return pl.BlockSpec((None,) + tail, lambda s, pt: (s // n_steps,) + (0,) * len(tail))

    hbm = pl.BlockSpec(memory_space=pl.ANY)
    in_specs = [batch_spec(n_rows_a, KV_LORA), batch_spec(n_rows_a, ROPE_HD), batch_spec(FOX_KV_HEADS, n_rows_f, FOX_HD),
                batch_spec(page, KV_LORA), batch_spec(ROPE_HD, page), batch_spec(page, FOX_KV_W),
                batch_spec(page, FOX_KV_W), batch_spec(FOX_HEADS, page), hbm, hbm, hbm, hbm, hbm]
    scratch = [
        pltpu.VMEM((2, tk, KV_LORA), F32), pltpu.VMEM((2, ROPE_HD, tk), F32),
        pltpu.VMEM((2, FOX_KV_HEADS * tk, FOX_HD), F32), pltpu.VMEM((2, FOX_KV_HEADS * tk, FOX_HD), F32),
        pltpu.VMEM((2, P * FOX_HEADS, page), F32), pltpu.SemaphoreType.DMA((2, 5)),
        pltpu.VMEM((tk, KV_LORA), BF), pltpu.VMEM((ROPE_HD, tk), BF), pltpu.VMEM((tk, FOX_KV_W), BF),
        pltpu.VMEM((tk, FOX_KV_W), BF), pltpu.VMEM((FOX_HEADS, tk), F32), pltpu.VMEM((FOX_HEADS, page), F32),
        pltpu.VMEM((n_rows_a, LANES), F32), pltpu.VMEM((n_rows_a, LANES), F32), pltpu.VMEM((n_rows_a, KV_LORA), F32),
        pltpu.VMEM((FOX_KV_HEADS, n_rows_f, LANES), F32), pltpu.VMEM((FOX_KV_HEADS, n_rows_f, LANES), F32),
        pltpu.VMEM((FOX_KV_HEADS, n_rows_f, FOX_HD), F32),
    ]
    grid_spec = pltpu.PrefetchScalarGridSpec(
        num_scalar_prefetch=1,
        grid=(n_total,),
        in_specs=in_specs,
        out_specs=[batch_spec(n_rows_a, KV_LORA), batch_spec(FOX_KV_HEADS, n_rows_f, FOX_HD)],
        scratch_shapes=scratch,
    )
    return pl.pallas_call(
        functools.partial(_decode_kernel, P=P, n_steps=n_steps, n_total=n_total, page=page, t_new=t_new),
        grid_spec=grid_spec,
        out_shape=[jax.ShapeDtypeStruct((B, n_rows_a, KV_LORA), BF),
                   jax.ShapeDtypeStruct((B, FOX_KV_HEADS, n_rows_f, FOX_HD), BF)],
        compiler_params=_params(1),
        name="decode",
    )(page_table.reshape(-1), qa, qr, qf, ckv_new, kr_new, fk_new, fv_new, lf_new,
      ckv_pool, kr_pool, fk_pool, fv_pool, lf_pool)


def _merge_kernel(oa_ref, of_ref, om_ref, ga_ref, gf_ref, b0_ref, b1_ref, b2_ref, wa_ref, wf_ref, wm_ref, o_ref,
                  acta_ref, actf_ref):
    @pl.when(pl.program_id(1) == 0)
    def _():
        ga = ga_ref[...].astype(F32)
        gf = gf_ref[...].astype(F32)
        acta_ref[...] = (oa_ref[...].astype(F32) * (ga * jax.nn.sigmoid(ga))).astype(BF)
        actf_ref[...] = (of_ref[...].astype(F32) * (gf * jax.nn.sigmoid(gf))).astype(BF)

    a = _dot(acta_ref[...], wa_ref[...])
    f = _dot(actf_ref[...], wf_ref[...])
    m = _dot(om_ref[...], wm_ref[...])
    merged = (jax.nn.sigmoid(b0_ref[...].astype(F32)) * a + jax.nn.sigmoid(b1_ref[...].astype(F32)) * f
              + jax.nn.sigmoid(b2_ref[...].astype(F32)) * m)
    o_ref[...] = merged.astype(o_ref.dtype)


def _merge(oa, of, om, hb, wa, wf, wm):
    M = oa.shape[0]
    tm = _pick(M, 512)
    tn = 512
    nb = D_MODEL // tn
    row = lambda c: (lambda i, j: (i, c))
    return pl.pallas_call(
        _merge_kernel,
        grid=(M // tm, nb),
        in_specs=[
            pl.BlockSpec((tm, MLA_W), row(0)),
            pl.BlockSpec((tm, FOX_W), row(0)),
            pl.BlockSpec((tm, MEM_W), row(0)),
            pl.BlockSpec((tm, MLA_W), row(B_GA // MLA_W)),
            pl.BlockSpec((tm, FOX_W), row(B_GF // FOX_W)),
            pl.BlockSpec((tm, tn), lambda i, j: (i, B_BR // tn + j)),
            pl.BlockSpec((tm, tn), lambda i, j: (i, B_BR // tn + nb + j)),
            pl.BlockSpec((tm, tn), lambda i, j: (i, B_BR // tn + 2 * nb + j)),
            pl.BlockSpec((MLA_W, tn), lambda i, j: (0, j)),
            pl.BlockSpec((FOX_W, tn), lambda i, j: (0, j)),
            pl.BlockSpec((MEM_W, tn), lambda i, j: (0, j)),
        ],
        out_specs=pl.BlockSpec((tm, tn), lambda i, j: (i, j)),
        out_shape=jax.ShapeDtypeStruct((M, D_MODEL), BF),
        scratch_shapes=[pltpu.VMEM((tm, MLA_W), BF), pltpu.VMEM((tm, FOX_W), BF)],
        compiler_params=_params(2),
        name="merge",
    )(oa, of, om, hb, hb, hb, hb, hb, wa, wf, wm)


def _out_ln_kernel(mg_ref, w_ref, x_ref, g_ref, b_ref, o_ref, *, alpha):
    y = _dot(mg_ref[...], w_ref[...])
    z = alpha * x_ref[...] + y
    mu = jnp.mean(z, axis=1, keepdims=True)
    zc = z - mu
    var = jnp.mean(zc * zc, axis=1, keepdims=True)
    o_ref[...] = zc * lax.rsqrt(var + LN_EPS) * g_ref[...] + b_ref[...]


def _out_ln(merged, w_out, x, ln_g, ln_b, alpha):
    M = x.shape[0]
    tm = _pick(M, 256)
    return pl.pallas_call(
        functools.partial(_out_ln_kernel, alpha=alpha),
        grid=(M // tm,),
        in_specs=[
            pl.BlockSpec((tm, D_MODEL), lambda i: (i, 0)),
            pl.BlockSpec((D_MODEL, D_MODEL), lambda i: (0, 0)),
            pl.BlockSpec((tm, D_MODEL), lambda i: (i, 0)),
            pl.BlockSpec((1, D_MODEL), lambda i: (0, 0)),
            pl.BlockSpec((1, D_MODEL), lambda i: (0, 0)),
        ],
        out_specs=pl.BlockSpec((tm, D_MODEL), lambda i: (i, 0)),
        out_shape=jax.ShapeDtypeStruct((M, D_MODEL), F32),
        compiler_params=_params(1),
        name="out_ln",
    )(merged, w_out, x, ln_g.reshape(1, -1), ln_b.reshape(1, -1))


def _rms(x, g):
    return x * lax.rsqrt(jnp.mean(x * x, axis=-1, keepdims=True) + RMS_EPS) * g


def _rope_tables(pos):
    half = ROPE_HD // 2
    inv = ROPE_THETA ** (-jnp.arange(half, dtype=F32) / half)
    ang = pos.astype(F32)[:, None] * inv[None, :]
    cos, sin = jnp.cos(ang), jnp.sin(ang)
    return jnp.concatenate([cos, cos], axis=-1), jnp.concatenate([-sin, sin], axis=-1)


def _swap_halves(w):
    half = ROPE_HD // 2
    return jnp.concatenate([w[..., half:], w[..., :half]], axis=-1)


def kernel(x_prompt, x_sample, mem_prompt, cache_mla_ckv, cache_mla_krope, cache_fox_k, cache_fox_v, cache_fox_logf,
           cache_mem_k, cache_mem_v, page_table, w_in, b_fox_f, q_norm_g, kv_norm_g, w_uq, w_uk, w_uv, w_mem_kv,
           w_br_mla, w_br_fox, w_br_mem, w_out, ln_g, ln_b):
    depth = w_in.shape[0]
    assert depth == 1 and x_prompt.shape[0] == 1
    alpha = (2 * depth) ** 0.25
    T = x_prompt.shape[1]
    B, t_new, _ = x_sample.shape
    Ts = B * t_new
    n_pool, page = cache_mla_ckv.shape[1], cache_mla_ckv.shape[2]
    n_pages = page_table.shape[1]
    past = n_pages * page
    n_mem = mem_prompt.shape[1]

    w = w_in[0]
    o_q, o_c, o_kr, o_ga, o_fq, o_fk, o_fv, o_fl, o_gf, o_mq, o_br = (
        0, 512, 1024, 1088, 3136, 5184, 5440, 5696, 5712, 7760, 8272)
    kr_w = w[:, o_kr:o_kr + ROPE_HD]
    w_a = jnp.concatenate([
        w[:, o_q:o_kr], kr_w, _swap_halves(kr_w), w[:, o_fk:o_fl], w[:, o_fl:o_fl + FOX_HEADS],
        jnp.zeros((D_MODEL, A_END - A_FLOG - FOX_HEADS), F32)], axis=1).astype(BF)
    w_b = jnp.concatenate([w[:, o_ga:o_fq], w[:, o_gf:o_mq], w[:, o_br:], w[:, o_fq:o_fk], w[:, o_mq:o_br]],
                          axis=1).astype(BF)
    uq = w_uq[0].reshape(Q_LORA, MLA_HEADS, NOPE_HD + ROPE_HD)
    uq_r = uq[:, :, NOPE_HD:]
    w_q = jnp.concatenate([uq[:, :, :NOPE_HD].reshape(Q_LORA, -1), uq_r.reshape(Q_LORA, -1),
                           _swap_halves(uq_r).reshape(Q_LORA, -1)], axis=1).astype(BF)
    w_kv = jnp.concatenate([w_uk[0].reshape(KV_LORA, -1), w_uv[0].reshape(KV_LORA, -1)], axis=1).astype(BF)
    w_uk_t = w_uk[0].transpose(1, 2, 0).astype(BF)
    w_uv_h = w_uv[0].transpose(1, 0, 2).astype(BF)

    x_all = jnp.concatenate([x_prompt[0], x_sample.reshape(Ts, D_MODEL)], axis=0)
    ha = _mm(x_all, w_a, F32, 1024, 896, "proj_a")
    hb = _mm(x_all, w_b, BF, 1024, 512, "proj_b")

    pos = jnp.concatenate([jnp.arange(T, dtype=jnp.int32),
                           jnp.tile(past + jnp.arange(t_new, dtype=jnp.int32), B)])
    cos2, sin2 = _rope_tables(pos)
    qn = _rms(ha[:, A_QLAT:A_CKV], q_norm_g[0])
    ckv = _rms(ha[:, A_CKV:A_KR], kv_norm_g[0])
    krope = ha[:, A_KR:A_KRS] * cos2 + ha[:, A_KRS:A_FK] * sin2
    fox_k = ha[:, A_FK:A_FV]
    fox_v = ha[:, A_FV:A_FLOG]
    log_f = jax.nn.log_sigmoid(ha[:, A_FLOG:A_FLOG + FOX_HEADS] + b_fox_f[0])

    q_nope = _mm(qn, w_q[:, :MLA_W], BF, 1024, 1024, "proj_q_nope")
    hq = _mm(qn, w_q[:, MLA_W:], F32, 1024, 1024, "proj_q_rope")
    q_rope = (hq[:, :MLA_HEADS * ROPE_HD] * jnp.tile(cos2, (1, MLA_HEADS))
              + hq[:, MLA_HEADS * ROPE_HD:] * jnp.tile(sin2, (1, MLA_HEADS))).astype(BF)

    kv = _mm(ckv[:T], w_kv, BF, 1024, 1024, "proj_kv")
    o_a_p = _mla_prompt(q_nope, q_rope, kv, krope[:T].astype(BF), T)
    fcum = jnp.cumsum(log_f[:T], axis=0) * LOG2E
    o_f_p = _fox_prompt(hb, fox_k[:T].astype(BF), fox_v[:T].astype(BF), fcum, T)
    mkv = _mm(mem_prompt[0], w_mem_kv[0].astype(BF), F32, 256, 512, "proj_mem")
    mq = hb[:, B_MQ:B_END]
    o_m_p = _mem_attend(mq[:T][None], mkv[None, :, :MEM_W], mkv[None, :, MEM_W:], 512, BF, "mem_prompt")[0]

    q_abs = _headwise_mm(q_nope[T:], w_uk_t, BF, 1024, "q_absorb")
    qa = q_abs.reshape(B, t_new * MLA_HEADS, KV_LORA)
    qr = q_rope[T:].reshape(B, t_new * MLA_HEADS, ROPE_HD)
    qf = hb[T:, B_FQ:B_MQ].reshape(B, t_new, FOX_KV_HEADS, FOX_GROUP, FOX_HD).transpose(0, 2, 1, 3, 4)
    qf = qf.reshape(B, FOX_KV_HEADS, t_new * FOX_GROUP, FOX_HD)
    def new_page(a):
        a = a.reshape(B, t_new, a.shape[-1])
        return jnp.pad(a, ((0, 0), (0, page - t_new), (0, 0)))

    o_lat, o_f_s = _decode(
        page_table, qa, qr, qf,
        cache_mla_ckv[0], cache_mla_krope[0].transpose(0, 2, 1),
        cache_fox_k[0].reshape(n_pool, page * FOX_KV_HEADS, FOX_HD),
        cache_fox_v[0].reshape(n_pool, page * FOX_KV_HEADS, FOX_HD), cache_fox_logf[0].transpose(0, 2, 1),
        new_page(ckv[T:]).astype(BF), new_page(krope[T:]).astype(BF).transpose(0, 2, 1),
        new_page(fox_k[T:]).astype(BF), new_page(fox_v[T:]).astype(BF), new_page(log_f[T:]).transpose(0, 2, 1))
    o_a_s = _headwise_mm(o_lat.reshape(Ts, MLA_HEADS * KV_LORA), w_uv_h, BF, 1024, "v_up")
    o_f_s = o_f_s.reshape(B, FOX_KV_HEADS, t_new, FOX_GROUP, FOX_HD).transpose(0, 2, 1, 3, 4).reshape(Ts, FOX_W)
    o_m_s = _mem_attend(mq[T:].astype(F32).reshape(B, t_new, MEM_W),
                        cache_mem_k[0].reshape(B, n_mem * MEM_HEADS, MEM_HD),
                        cache_mem_v[0].reshape(B, n_mem * MEM_HEADS, MEM_HD), t_new, F32, "mem_sample")
    o_m_s = o_m_s.reshape(Ts, MEM_W).astype(BF)

    merged = _merge(jnp.concatenate([o_a_p, o_a_s]), jnp.concatenate([o_f_p, o_f_s]),
                    jnp.concatenate([o_m_p, o_m_s]), hb,
                    w_br_mla[0].astype(BF), w_br_fox[0].astype(BF), w_br_mem[0].astype(BF))
    y = _out_ln(merged, w_out[0].astype(BF), x_all, ln_g[0], ln_b[0], alpha)

    mk = mkv[:, :MEM_W].reshape(1, 1, n_mem, MEM_HEADS, MEM_HD)
    mv = mkv[:, MEM_W:].reshape(1, 1, n_mem, MEM_HEADS, MEM_HD)
    return (
        y[:T][None], y[T:].reshape(B, t_new, D_MODEL),
        ckv[:T][None, None], krope[:T][None, None],
        fox_k[:T].reshape(1, 1, T, FOX_KV_HEADS, FOX_HD), fox_v[:T].reshape(1, 1, T, FOX_KV_HEADS, FOX_HD),
        log_f[:T][None, None], mk, mv,
        ckv[T:].reshape(1, B, t_new, KV_LORA), krope[T:].reshape(1, B, t_new, ROPE_HD),
        fox_k[T:].reshape(1, B, t_new, FOX_KV_HEADS, FOX_HD), fox_v[T:].reshape(1, B, t_new, FOX_KV_HEADS, FOX_HD),
        log_f[T:].reshape(1, B, t_new, FOX_HEADS),
    )
```

```python
import functools

import jax
import jax.numpy as jnp
from jax import lax
from jax.experimental import pallas as pl
from jax.experimental.pallas import tpu as pltpu

BF = jnp.bfloat16
F32 = jnp.float32

D_MODEL = 2048
MLA_HEADS = 16
Q_LORA = 512
KV_LORA = 512
NOPE_HD = 128
ROPE_HD = 64
V_HD = 128
MLA_W = MLA_HEADS * V_HD
MLA_SCALE = (NOPE_HD + ROPE_HD) ** -0.5
ROPE_THETA = 10000.0
FOX_HEADS = 16
FOX_KV_HEADS = 2
FOX_HD = 128
FOX_GROUP = FOX_HEADS // FOX_KV_HEADS
FOX_W = FOX_HEADS * FOX_HD
FOX_KV_W = FOX_KV_HEADS * FOX_HD
FOX_SCALE = FOX_HD ** -0.5
MEM_HEADS = 4
MEM_HD = 128
MEM_W = MEM_HEADS * MEM_HD
MEM_SCALE = MEM_HD ** -0.5
N_BRANCH = 3
RMS_EPS = 1e-6
LN_EPS = 1e-5

NEG = -1e30
LOG2E = 1.4426950408889634
VMEM_LIMIT = 56 * 1024 * 1024
PAGES_PER_STEP = 32
LANES = 128
FLASH_TQ, FLASH_TK = 1024, 512

A_QLAT, A_CKV, A_KR, A_KRS, A_FK, A_FV, A_FLOG, A_END = 0, 512, 1024, 1088, 1152, 1408, 1664, 1792
B_GA, B_GF, B_BR, B_FQ, B_MQ, B_END = 0, 2048, 4096, 10240, 12288, 12800


def _pick(n, target, align=8):
    if n <= target:
        return n
    for d in range(target, 0, -1):
        if n % d == 0 and d % align == 0:
            return d
    raise ValueError((n, target, align))


def _params(n_axes):
    return pltpu.CompilerParams(dimension_semantics=("arbitrary",) * n_axes, vmem_limit_bytes=VMEM_LIMIT)


def _dot_nt(a, b):
    return lax.dot_general(a, b, (((1,), (1,)), ((), ())), preferred_element_type=F32)


def _dot(a, b):
    return jnp.dot(a, b, preferred_element_type=F32)


def _mm_kernel(a_ref, b_ref, o_ref, abf_ref):
    @pl.when(pl.program_id(1) == 0)
    def _():
        abf_ref[...] = a_ref[...].astype(BF)

    o_ref[...] = _dot(abf_ref[...], b_ref[...]).astype(o_ref.dtype)


def _mm(a, b, out_dtype, tm, tn, name):
    M, K = a.shape
    N = b.shape[1]
    tm = _pick(M, tm)
    tn = _pick(N, tn, 128)
    return pl.pallas_call(
        _mm_kernel,
        grid=(M // tm, N // tn),
        in_specs=[pl.BlockSpec((tm, K), lambda i, j: (i, 0)), pl.BlockSpec((K, tn), lambda i, j: (0, j))],
        out_specs=pl.BlockSpec((tm, tn), lambda i, j: (i, j)),
        out_shape=jax.ShapeDtypeStruct((M, N), out_dtype),
        scratch_shapes=[pltpu.VMEM((tm, K), BF)],
        compiler_params=_params(2),
        name=name,
    )(a, b)


def _headwise_kernel(a_ref, w_ref, o_ref):
    o_ref[...] = _dot(a_ref[...], w_ref[...]).astype(o_ref.dtype)


def _headwise_mm(a, w, out_dtype, tm, name):
    M = a.shape[0]
    H, K, N = w.shape
    tm = _pick(M, tm)
    return pl.pallas_call(
        _headwise_kernel,
        grid=(H, M // tm),
        in_specs=[pl.BlockSpec((tm, K), lambda h, i: (i, h)), pl.BlockSpec((None, K, N), lambda h, i: (h, 0, 0))],
        out_specs=pl.BlockSpec((tm, N), lambda h, i: (i, h)),
        out_shape=jax.ShapeDtypeStruct((M, H * N), out_dtype),
        compiler_params=_params(2),
        name=name,
    )(a, w)


def _rep(x, n):
    return x if n == LANES else jnp.tile(x, (1, n // LANES))


def _online_softmax(slot, s, v, first=False):
    m_ref, l_ref, acc_ref = slot
    rows, keys = s.shape
    m_blk = jnp.broadcast_to(jnp.max(s, axis=1, keepdims=True), (rows, LANES))
    if first:
        m_new = m_blk
    else:
        m_prev = m_ref[...]
        m_new = jnp.maximum(m_prev, m_blk)
        alpha = jnp.exp2(m_prev - m_new)
    p = jnp.exp2(s - _rep(m_new, keys))
    l_blk = jnp.broadcast_to(jnp.sum(p, axis=1, keepdims=True), (rows, LANES))
    pv = _dot(p.astype(BF), v)
    if first:
        l_ref[...] = l_blk
        acc_ref[...] = pv
    else:
        l_ref[...] = alpha * l_ref[...] + l_blk
        acc_ref[...] = _rep(alpha, acc_ref.shape[-1]) * acc_ref[...] + pv
    m_ref[...] = m_new


def _causal_flash(i, tq, tk, scores, values, slot):
    r = tq // tk
    row = lax.broadcasted_iota(jnp.int32, (tq, tk), 0)
    col = lax.broadcasted_iota(jnp.int32, (tq, tk), 1)
    for d in range(r):
        s = jnp.where(col + d * tk <= row, scores(r * i + d), NEG)
        _online_softmax(slot, s, values(r * i + d), first=(d == 0))

    def body(j, carry):
        _online_softmax(slot, scores(j), values(j))
        return carry

    lax.fori_loop(0, r * i, body, 0)


def _key_rows(j, tk):
    return pl.ds(pl.multiple_of(j * tk, tk), tk)


def _mla_prompt_kernel(qn_ref, qr_ref, kn_ref, kr_ref, v_ref, o_ref, kcat_ref, qcat_ref, m_ref, l_ref, acc_ref, *,
                       tq, tk, T):
    i = pl.program_id(1)
    qk_dim = NOPE_HD + ROPE_HD

    @pl.when(i == 0)
    def _():
        for c in range(T // tk):
            rows = slice(c * tk, (c + 1) * tk)
            for hh in range(2):
                kcat_ref[hh, rows, 0:NOPE_HD] = kn_ref[rows, hh * NOPE_HD:(hh + 1) * NOPE_HD]
                kcat_ref[hh, rows, NOPE_HD:qk_dim] = kr_ref[rows, :]

    for hh in range(2):
        qcat_ref[:, 0:NOPE_HD] = qn_ref[:, hh * NOPE_HD:(hh + 1) * NOPE_HD]
        qcat_ref[:, NOPE_HD:qk_dim] = qr_ref[:, hh * ROPE_HD:(hh + 1) * ROPE_HD]

        def scores(j, hh=hh):
            return _dot_nt(qcat_ref[...], kcat_ref[hh, _key_rows(j, tk), :]) * (MLA_SCALE * LOG2E)

        def values(j, hh=hh):
            return v_ref[_key_rows(j, tk), hh * V_HD:(hh + 1) * V_HD]

        _causal_flash(i, tq, tk, scores, values, (m_ref, l_ref, acc_ref))
        o_ref[:, hh * V_HD:(hh + 1) * V_HD] = (acc_ref[...] / l_ref[...]).astype(o_ref.dtype)


def _flash_tiles(T):
    tk = _pick(T, FLASH_TK)
    tq = _pick(T, FLASH_TQ, tk)
    return tq, tk


def _mla_prompt(qn, qr, kv, kr, T):
    tq, tk = _flash_tiles(T)
    n_kcols = MLA_W // (2 * NOPE_HD)
    return pl.pallas_call(
        functools.partial(_mla_prompt_kernel, tq=tq, tk=tk, T=T),
        grid=(MLA_HEADS // 2, T // tq),
        in_specs=[
            pl.BlockSpec((tq, 2 * NOPE_HD), lambda h, i: (i, h)),
            pl.BlockSpec((tq, 2 * ROPE_HD), lambda h, i: (i, h)),
            pl.BlockSpec((T, 2 * NOPE_HD), lambda h, i: (0, h)),
            pl.BlockSpec((T, ROPE_HD), lambda h, i: (0, 0)),
            pl.BlockSpec((T, 2 * V_HD), lambda h, i: (0, n_kcols + h)),
        ],
        out_specs=pl.BlockSpec((tq, 2 * V_HD), lambda h, i: (i, h)),
        out_shape=jax.ShapeDtypeStruct((T, MLA_W), BF),
        scratch_shapes=[pltpu.VMEM((2, T, NOPE_HD + ROPE_HD), BF), pltpu.VMEM((tq, NOPE_HD + ROPE_HD), BF),
                        pltpu.VMEM((tq, LANES), F32), pltpu.VMEM((tq, LANES), F32), pltpu.VMEM((tq, V_HD), F32)],
        compiler_params=_params(2),
        name="mla_prompt",
    )(qn, qr, kv, kr, kv)


def _fox_prompt_kernel(q_ref, k_ref, v_ref, fq_ref, fk_ref, o_ref, fqb_ref, m_ref, l_ref, acc_ref, *, tq, tk):
    i = pl.program_id(1)
    fq_all = fq_ref[...]
    for h in range(FOX_GROUP):
        sl = slice(h * FOX_HD, (h + 1) * FOX_HD)
        fqb_ref[...] = jnp.broadcast_to(fq_all[:, h:h + 1], (tq, LANES))

        def scores(j, h=h, sl=sl):
            fk = fk_ref[h, pl.ds(j, 1), :]
            return (_dot_nt(q_ref[:, sl], k_ref[_key_rows(j, tk), :]) * (FOX_SCALE * LOG2E)
                    + (_rep(fqb_ref[...], tk) - fk))

        def values(j):
            return v_ref[_key_rows(j, tk), :]

        _causal_flash(i, tq, tk, scores, values, (m_ref, l_ref, acc_ref))
        o_ref[:, sl] = (acc_ref[...] / l_ref[...]).astype(o_ref.dtype)


def _fox_prompt(hb, k, v, fcum, T):
    tq, tk = _flash_tiles(T)
    gw = FOX_GROUP * FOX_HD
    fk3 = fcum.T.reshape(FOX_KV_HEADS, FOX_GROUP, T // tk, tk)
    fq3 = fcum.reshape(T, FOX_KV_HEADS, FOX_GROUP).transpose(1, 0, 2)
    return pl.pallas_call(
        functools.partial(_fox_prompt_kernel, tq=tq, tk=tk),
        grid=(FOX_KV_HEADS, T // tq),
        in_specs=[
            pl.BlockSpec((tq, gw), lambda g, i: (i, B_FQ // gw + g)),
            pl.BlockSpec((T, FOX_HD), lambda g, i: (0, g)),
            pl.BlockSpec((T, FOX_HD), lambda g, i: (0, g)),
            pl.BlockSpec((None, tq, FOX_GROUP), lambda g, i: (g, i, 0)),
            pl.BlockSpec((None, FOX_GROUP, T // tk, tk), lambda g, i: (g, 0, 0, 0)),
        ],
        out_specs=pl.BlockSpec((tq, gw), lambda g, i: (i, g)),
        out_shape=jax.ShapeDtypeStruct((T, FOX_W), BF),
        scratch_shapes=[pltpu.VMEM((tq, LANES), F32), pltpu.VMEM((tq, LANES), F32), pltpu.VMEM((tq, LANES), F32),
                        pltpu.VMEM((tq, FOX_HD), F32)],
        compiler_params=_params(2),
        name="fox_prompt",
    )(hb, k, v, fq3, fk3)


def _mem_kernel(q_ref, k_ref, v_ref, o_ref, *, n_mem, token_major):
    for h in range(MEM_HEADS):
        sl = slice(h * MEM_HD, (h + 1) * MEM_HD)
        q = q_ref[:, sl].astype(BF)
        if token_major:
            k = k_ref[pl.ds(h, n_mem, stride=MEM_HEADS), :].astype(BF)
            v = v_ref[pl.ds(h, n_mem, stride=MEM_HEADS), :].astype(BF)
        else:
            k = k_ref[:, sl].astype(BF)
            v = v_ref[:, sl].astype(BF)
        s = _dot_nt(q, k) * (MEM_SCALE * LOG2E)
        m = jnp.max(s, axis=1, keepdims=True)
        p = jnp.exp2(s - m)
        l = jnp.sum(p, axis=1, keepdims=True)
        o_ref[:, sl] = (_dot(p.astype(BF), v) / l).astype(o_ref.dtype)


def _mem_attend(q, k, v, tq, out_dtype, name):
    B, Tq, _ = q.shape
    token_major = k.shape[2] == MEM_HD
    n_mem = k.shape[1] // MEM_HEADS if token_major else k.shape[1]
    tq = _pick(Tq, tq)
    kv_spec = pl.BlockSpec((None,) + k.shape[1:], lambda b, i: (b, 0, 0))
    return pl.pallas_call(
        functools.partial(_mem_kernel, n_mem=n_mem, token_major=token_major),
        grid=(B, Tq // tq),
        in_specs=[pl.BlockSpec((None, tq, MEM_W), lambda b, i: (b, i, 0)), kv_spec, kv_spec],
        out_specs=pl.BlockSpec((None, tq, MEM_W), lambda b, i: (b, i, 0)),
        out_shape=jax.ShapeDtypeStruct((B, Tq, MEM_W), out_dtype),
        compiler_params=_params(2),
        name=name,
    )(q, k, v)


def _decode_kernel(pt_ref, qa_ref, qr_ref, qf_ref, ckvn_ref, krn_ref, fkn_ref, fvn_ref, lfn_ref,
                   ckv_hbm, kr_hbm, fk_hbm, fv_hbm, lf_hbm, oa_ref, of_ref,
                   ckv_buf, kr_buf, fk_buf, fv_buf, lf_buf, sem, kall, krall, fkall, fvall, bias, fbase,
                   m_a, l_a, acc_a, m_f, l_f, acc_f, *, P, n_steps, n_total, page, t_new):
    step = pl.program_id(0)
    j = step % n_steps
    slot = step % 2
    tk = P * page
    n_rows_a = t_new * MLA_HEADS
    n_rows_f = t_new * FOX_GROUP

    def page_copies(step_, slot_):
        copies = []
        for p in range(P):
            idx = pt_ref[step_ * P + p]
            copies += [
                pltpu.make_async_copy(ckv_hbm.at[idx], ckv_buf.at[slot_, pl.ds(p * page, page), :], sem.at[slot_, 0]),
                pltpu.make_async_copy(kr_hbm.at[idx], kr_buf.at[slot_, :, pl.ds(p * page, page)], sem.at[slot_, 1]),
                pltpu.make_async_copy(fk_hbm.at[idx], fk_buf.at[slot_, pl.ds(p * FOX_KV_HEADS * page, FOX_KV_HEADS * page), :],
                                      sem.at[slot_, 2]),
                pltpu.make_async_copy(fv_hbm.at[idx], fv_buf.at[slot_, pl.ds(p * FOX_KV_HEADS * page, FOX_KV_HEADS * page), :],
                                      sem.at[slot_, 3]),
                pltpu.make_async_copy(lf_hbm.at[idx], lf_buf.at[slot_, pl.ds(p * FOX_HEADS, FOX_HEADS), :],
                                      sem.at[slot_, 4]),
            ]
        return copies

    @pl.when(step == 0)
    def _():
        for c in page_copies(0, 0):
            c.start()

    @pl.when(step + 1 < n_total)
    def _():
        for c in page_copies(step + 1, 1 - slot):
            c.start()

    @pl.when(j == 0)
    def _():
        m_a[...] = jnp.full(m_a.shape, NEG, F32)
        l_a[...] = jnp.zeros(l_a.shape, F32)
        acc_a[...] = jnp.zeros(acc_a.shape, F32)
        m_f[...] = jnp.full(m_f.shape, NEG, F32)
        l_f[...] = jnp.zeros(l_f.shape, F32)
        acc_f[...] = jnp.zeros(acc_f.shape, F32)
        fbase[...] = jnp.zeros(fbase.shape, F32)

    slots = [(m_a, l_a, acc_a)] + [(m_f.at[g], l_f.at[g], acc_f.at[g]) for g in range(FOX_KV_HEADS)]
    tri = (lax.broadcasted_iota(jnp.int32, (page, page), 0) <= lax.broadcasted_iota(jnp.int32, (page, page), 1)
           ).astype(F32)

    def page_cumsum(lf):
        return jnp.dot(lf, tri, precision=lax.Precision.HIGHEST, preferred_element_type=F32)

    def attend(k, kr_t, fk, fv, b, mask_a, mask_f):
        scores = [(_dot_nt(qa_ref[...], k) + _dot(qr_ref[...], kr_t)) * (MLA_SCALE * LOG2E)]
        values = [k]
        for g in range(FOX_KV_HEADS):
            sl = slice(g * FOX_HD, (g + 1) * FOX_HD)
            bg = b[g * FOX_GROUP:(g + 1) * FOX_GROUP, :]
            scores.append(_dot_nt(qf_ref[g], fk[:, sl]) * (FOX_SCALE * LOG2E) - jnp.concatenate([bg] * t_new, axis=0))
            values.append(fv[:, sl])
        if mask_a is not None:
            scores = [jnp.where(mask_a, scores[0], NEG)] + [jnp.where(mask_f, s, NEG) for s in scores[1:]]
        for slot, s, v in zip(slots, scores, values):
            _online_softmax(slot, s, v)

    for a, buf in enumerate((ckv_buf, kr_buf, fk_buf, fv_buf, lf_buf)):
        pltpu.make_async_copy(buf.at[slot], buf.at[slot], sem.at[slot, a]).wait()

    kall[...] = ckv_buf[slot].astype(BF)
    krall[...] = kr_buf[slot].astype(BF)
    for g in range(FOX_KV_HEADS):
        sl = slice(g * FOX_HD, (g + 1) * FOX_HD)
        fkall[:, sl] = fk_buf[slot, pl.ds(g, tk, stride=FOX_KV_HEADS), :].astype(BF)
        fvall[:, sl] = fv_buf[slot, pl.ds(g, tk, stride=FOX_KV_HEADS), :].astype(BF)
    fl = page_cumsum(lf_buf[slot])
    base = fbase[...]
    for p in range(P):
        fl_p = fl[p * FOX_HEADS:(p + 1) * FOX_HEADS, :]
        bias[:, p * page:(p + 1) * page] = (base + fl_p) * LOG2E
        base = base + jnp.broadcast_to(fl_p[:, page - 1:page], base.shape)
    fbase[...] = base
    attend(kall[...], krall[...], fkall[...], fvall[...], bias[...], None, None)

    @pl.when(j == n_steps - 1)
    def _():
        col_a = lax.broadcasted_iota(jnp.int32, (n_rows_a, page), 1)
        tok_a = lax.broadcasted_iota(jnp.int32, (n_rows_a, page), 0) // MLA_HEADS
        col_f = lax.broadcasted_iota(jnp.int32, (n_rows_f, page), 1)
        tok_f = lax.broadcasted_iota(jnp.int32, (n_rows_f, page), 0) // FOX_GROUP
        b_new = (fbase[...] + page_cumsum(lfn_ref[...])) * LOG2E
        attend(ckvn_ref[...], krn_ref[...], fkn_ref[...], fvn_ref[...], b_new, col_a <= tok_a, col_f <= tok_f)
        oa_ref[...] = (acc_a[...] / _rep(l_a[...], KV_LORA)).astype(oa_ref.dtype)
        for g in range(FOX_KV_HEADS):
            of_ref[g] = (acc_f[g] / l_f[g]).astype(of_ref.dtype)


def _decode(page_table, qa, qr, qf, ckv_pool, kr_pool, fk_pool, fv_pool, lf_pool, ckv_new, kr_new, fk_new, fv_new,
            lf_new):
    B, n_pages = page_table.shape
    page = ckv_pool.shape[1]
    P = _pick(n_pages, PAGES_PER_STEP, 1)
    n_steps = n_pages // P
    n_total = B * n_steps
    n_rows_a = qa.shape[1]
    t_new = n_rows_a // MLA_HEADS
    n_rows_f = qf.shape[2]
    tk = P * page

    def batch_spec(*tail):
        return pl.BlockSpec((None,) + tail, lambda s, pt: (s // n_steps,) + (0,) * len(tail))

    hbm = pl.BlockSpec(memory_space=pl.ANY)
    in_specs = [batch_spec(n_rows_a, KV_LORA), batch_spec(n_rows_a, ROPE_HD), batch_spec(FOX_KV_HEADS, n_rows_f, FOX_HD),
                batch_spec(page, KV_LORA), batch_spec(ROPE_HD, page), batch_spec(page, FOX_KV_W),
                batch_spec(page, FOX_KV_W), batch_spec(FOX_HEADS, page), hbm, hbm, hbm, hbm, hbm]
    scratch = [
        pltpu.VMEM((2, tk, KV_LORA), F32), pltpu.VMEM((2, ROPE_HD, tk), F32),
        pltpu.VMEM((2, FOX_KV_HEADS * tk, FOX_HD), F32), pltpu.VMEM((2, FOX_KV_HEADS * tk, FOX_HD), F32),
        pltpu.VMEM((2, P * FOX_HEADS, page), F32), pltpu.SemaphoreType.DMA((2, 5)),
        pltpu.VMEM((tk, KV_LORA), BF), pltpu.VMEM((ROPE_HD, tk), BF), pltpu.VMEM((tk, FOX_KV_W), BF),
        pltpu.VMEM((tk, FOX_KV_W), BF), pltpu.VMEM((FOX_HEADS, tk), F32), pltpu.VMEM((FOX_HEADS, page), F32),
        pltpu.VMEM((n_rows_a, LANES), F32), pltpu.VMEM((n_rows_a, LANES), F32), pltpu.VMEM((n_rows_a, KV_LORA), F32),
        pltpu.VMEM((FOX_KV_HEADS, n_rows_f, LANES), F32), pltpu.VMEM((FOX_KV_HEADS, n_rows_f, LANES), F32),
        pltpu.VMEM((FOX_KV_HEADS, n_rows_f, FOX_HD), F32),
    ]
    grid_spec = pltpu.PrefetchScalarGridSpec(
        num_scalar_prefetch=1,
        grid=(n_total,),
        in_specs=in_specs,
        out_specs=[batch_spec(n_rows_a, KV_LORA), batch_spec(FOX_KV_HEADS, n_rows_f, FOX_HD)],
        scratch_shapes=scratch,
    )
    return pl.pallas_call(
        functools.partial(_decode_kernel, P=P, n_steps=n_steps, n_total=n_total, page=page, t_new=t_new),
        grid_spec=grid_spec,
        out_shape=[jax.ShapeDtypeStruct((B, n_rows_a, KV_LORA), BF),
                   jax.ShapeDtypeStruct((B, FOX_KV_HEADS, n_rows_f, FOX_HD), BF)],
        compiler_params=_params(1),
        name="decode",
    )(page_table.reshape(-1), qa, qr, qf, ckv_new, kr_new, fk_new, fv_new, lf_new,
      ckv_pool, kr_pool, fk_pool, fv_pool, lf_pool)


def _merge_kernel(oa_ref, of_ref, om_ref, ga_ref, gf_ref, b0_ref, b1_ref, b2_ref, wa_ref, wf_ref, wm_ref, o_ref,
                  acta_ref, actf_ref):
    @pl.when(pl.program_id(1) == 0)
    def _():
        ga = ga_ref[...].astype(F32)
        gf = gf_ref[...].astype(F32)
        acta_ref[...] = (oa_ref[...].astype(F32) * (ga * jax.nn.sigmoid(ga))).astype(BF)
        actf_ref[...] = (of_ref[...].astype(F32) * (gf * jax.nn.sigmoid(gf))).astype(BF)

    a = _dot(acta_ref[...], wa_ref[...])
    f = _dot(actf_ref[...], wf_ref[...])
    m = _dot(om_ref[...], wm_ref[...])
    merged = (jax.nn.sigmoid(b0_ref[...].astype(F32)) * a + jax.nn.sigmoid(b1_ref[...].astype(F32)) * f
              + jax.nn.sigmoid(b2_ref[...].astype(F32)) * m)
    o_ref[...] = merged.astype(o_ref.dtype)


def _merge(oa, of, om, hb, wa, wf, wm):
    M = oa.shape[0]
    tm = _pick(M, 512)
    tn = 512
    nb = D_MODEL // tn
    row = lambda c: (lambda i, j: (i, c))
    return pl.pallas_call(
        _merge_kernel,
        grid=(M // tm, nb),
        in_specs=[
            pl.BlockSpec((tm, MLA_W), row(0)),
            pl.BlockSpec((tm, FOX_W), row(0)),
            pl.BlockSpec((tm, MEM_W), row(0)),
            pl.BlockSpec((tm, MLA_W), row(B_GA // MLA_W)),
            pl.BlockSpec((tm, FOX_W), row(B_GF // FOX_W)),
            pl.BlockSpec((tm, tn), lambda i, j: (i, B_BR // tn + j)),
            pl.BlockSpec((tm, tn), lambda i, j: (i, B_BR // tn + nb + j)),
            pl.BlockSpec((tm, tn), lambda i, j: (i, B_BR // tn + 2 * nb + j)),
            pl.BlockSpec((MLA_W, tn), lambda i, j: (0, j)),
            pl.BlockSpec((FOX_W, tn), lambda i, j: (0, j)),
            pl.BlockSpec((MEM_W, tn), lambda i, j: (0, j)),
        ],
        out_specs=pl.BlockSpec((tm, tn), lambda i, j: (i, j)),
        out_shape=jax.ShapeDtypeStruct((M, D_MODEL), BF),
        scratch_shapes=[pltpu.VMEM((tm, MLA_W), BF), pltpu.VMEM((tm, FOX_W), BF)],
        compiler_params=_params(2),
        name="merge",
    )(oa, of, om, hb, hb, hb, hb, hb, wa, wf, wm)


def _out_ln_kernel(mg_ref, w_ref, x_ref, g_ref, b_ref, o_ref, *, alpha):
    y = _dot(mg_ref[...], w_ref[...])
    z = alpha * x_ref[...] + y
    mu = jnp.mean(z, axis=1, keepdims=True)
    zc = z - mu
    var = jnp.mean(zc * zc, axis=1, keepdims=True)
    o_ref[...] = zc * lax.rsqrt(var + LN_EPS) * g_ref[...] + b_ref[...]


def _out_ln(merged, w_out, x, ln_g, ln_b, alpha):
    M = x.shape[0]
    tm = _pick(M, 256)
    return pl.pallas_call(
        functools.partial(_out_ln_kernel, alpha=alpha),
        grid=(M // tm,),
        in_specs=[
            pl.BlockSpec((tm, D_MODEL), lambda i: (i, 0)),
            pl.BlockSpec((D_MODEL, D_MODEL), lambda i: (0, 0)),
            pl.BlockSpec((tm, D_MODEL), lambda i: (i, 0)),
            pl.BlockSpec((1, D_MODEL), lambda i: (0, 0)),
            pl.BlockSpec((1, D_MODEL), lambda i: (0, 0)),
        ],
        out_specs=pl.BlockSpec((tm, D_MODEL), lambda i: (i, 0)),
        out_shape=jax.ShapeDtypeStruct((M, D_MODEL), F32),
        compiler_params=_params(1),
        name="out_ln",
    )(merged, w_out, x, ln_g.reshape(1, -1), ln_b.reshape(1, -1))


def _rms(x, g):
    return x * lax.rsqrt(jnp.mean(x * x, axis=-1, keepdims=True) + RMS_EPS) * g


def _rope_tables(pos):
    half = ROPE_HD // 2
    inv = ROPE_THETA ** (-jnp.arange(half, dtype=F32) / half)
    ang = pos.astype(F32)[:, None] * inv[None, :]
    cos, sin = jnp.cos(ang), jnp.sin(ang)
    return jnp.concatenate([cos, cos], axis=-1), jnp.concatenate([-sin, sin], axis=-1)


def _swap_halves(w):
    half = ROPE_HD // 2
    return jnp.concatenate([w[..., half:], w[..., :half]], axis=-1)


def kernel(x_prompt, x_sample, mem_prompt, cache_mla_ckv, cache_mla_krope, cache_fox_k, cache_fox_v, cache_fox_logf,
           cache_mem_k, cache_mem_v, page_table, w_in, b_fox_f, q_norm_g, kv_norm_g, w_uq, w_uk, w_uv, w_mem_kv,
           w_br_mla, w_br_fox, w_br_mem, w_out, ln_g, ln_b):
    depth = w_in.shape[0]
    assert depth == 1 and x_prompt.shape[0] == 1
    alpha = (2 * depth) ** 0.25
    T = x_prompt.shape[1]
    B, t_new, _ = x_sample.shape
    Ts = B * t_new
    n_pool, page = cache_mla_ckv.shape[1], cache_mla_ckv.shape[2]
    n_pages = page_table.shape[1]
    past = n_pages * page
    n_mem = mem_prompt.shape[1]

    w = w_in[0]
    o_q, o_c, o_kr, o_ga, o_fq, o_fk, o_fv, o_fl, o_gf, o_mq, o_br = (
        0, 512, 1024, 1088, 3136, 5184, 5440, 5696, 5712, 7760, 8272)
    kr_w = w[:, o_kr:o_kr + ROPE_HD]
    w_a = jnp.concatenate([
        w[:, o_q:o_kr], kr_w, _swap_halves(kr_w), w[:, o_fk:o_fl], w[:, o_fl:o_fl + FOX_HEADS],
        jnp.zeros((D_MODEL, A_END - A_FLOG - FOX_HEADS), F32)], axis=1).astype(BF)
    w_b = jnp.concatenate([w[:, o_ga:o_fq], w[:, o_gf:o_mq], w[:, o_br:], w[:, o_fq:o_fk], w[:, o_mq:o_br]],
                          axis=1).astype(BF)
    uq = w_uq[0].reshape(Q_LORA, MLA_HEADS, NOPE_HD + ROPE_HD)
    uq_r = uq[:, :, NOPE_HD:]
    w_q = jnp.concatenate([uq[:, :, :NOPE_HD].reshape(Q_LORA, -1), uq_r.reshape(Q_LORA, -1),
                           _swap_halves(uq_r).reshape(Q_LORA, -1)], axis=1).astype(BF)
    w_kv = jnp.concatenate([w_uk[0].reshape(KV_LORA, -1), w_uv[0].reshape(KV_LORA, -1)], axis=1).astype(BF)
    w_uk_t = w_uk[0].transpose(1, 2, 0).astype(BF)
    w_uv_h = w_uv[0].transpose(1, 0, 2).astype(BF)

    x_all = jnp.concatenate([x_prompt[0], x_sample.reshape(Ts, D_MODEL)], axis=0)
    ha = _mm(x_all, w_a, F32, 1024, 896, "proj_a")
    hb = _mm(x_all, w_b, BF, 1024, 512, "proj_b")

    pos = jnp.concatenate([jnp.arange(T, dtype=jnp.int32),
                           jnp.tile(past + jnp.arange(t_new, dtype=jnp.int32), B)])
    cos2, sin2 = _rope_tables(pos)
    qn = _rms(ha[:, A_QLAT:A_CKV], q_norm_g[0])
    ckv = _rms(ha[:, A_CKV:A_KR], kv_norm_g[0])
    krope = ha[:, A_KR:A_KRS] * cos2 + ha[:, A_KRS:A_FK] * sin2
    fox_k = ha[:, A_FK:A_FV]
    fox_v = ha[:, A_FV:A_FLOG]
    log_f = jax.nn.log_sigmoid(ha[:, A_FLOG:A_FLOG + FOX_HEADS] + b_fox_f[0])

    q_nope = _mm(qn, w_q[:, :MLA_W], BF, 1024, 1024, "proj_q_nope")
    hq = _mm(qn, w_q[:, MLA_W:], F32, 1024, 1024, "proj_q_rope")
    q_rope = (hq[:, :MLA_HEADS * ROPE_HD] * jnp.tile(cos2, (1, MLA_HEADS))
              + hq[:, MLA_HEADS * ROPE_HD:] * jnp.tile(sin2, (1, MLA_HEADS))).astype(BF)

    kv = _mm(ckv[:T], w_kv, BF, 1024, 1024, "proj_kv")
    o_a_p = _mla_prompt(q_nope, q_rope, kv, krope[:T].astype(BF), T)
    fcum = jnp.cumsum(log_f[:T], axis=0) * LOG2E
    o_f_p = _fox_prompt(hb, fox_k[:T].astype(BF), fox_v[:T].astype(BF), fcum, T)
    mkv = _mm(mem_prompt[0], w_mem_kv[0].astype(BF), F32, 256, 512, "proj_mem")
    mq = hb[:, B_MQ:B_END]
    o_m_p = _mem_attend(mq[:T][None], mkv[None, :, :MEM_W], mkv[None, :, MEM_W:], 512, BF, "mem_prompt")[0]

    q_abs = _headwise_mm(q_nope[T:], w_uk_t, BF, 1024, "q_absorb")
    qa = q_abs.reshape(B, t_new * MLA_HEADS, KV_LORA)
    qr = q_rope[T:].reshape(B, t_new * MLA_HEADS, ROPE_HD)
    qf = hb[T:, B_FQ:B_MQ].reshape(B, t_new, FOX_KV_HEADS, FOX_GROUP, FOX_HD).transpose(0, 2, 1, 3, 4)
    qf = qf.reshape(B, FOX_KV_HEADS, t_new * FOX_GROUP, FOX_HD)
    def new_page(a):
        a = a.reshape(B, t_new, a.shape[-1])
        return jnp.pad(a, ((0, 0), (0, page - t_new), (0, 0)))

    o_lat, o_f_s = _decode(
        page_table, qa, qr, qf,
        cache_mla_ckv[0], cache_mla_krope[0].transpose(0, 2, 1),
        cache_fox_k[0].reshape(n_pool, page * FOX_KV_HEADS, FOX_HD),
        cache_fox_v[0].reshape(n_pool, page * FOX_KV_HEADS, FOX_HD), cache_fox_logf[0].transpose(0, 2, 1),
        new_page(ckv[T:]).astype(BF), new_page(krope[T:]).astype(BF).transpose(0, 2, 1),
        new_page(fox_k[T:]).astype(BF), new_page(fox_v[T:]).astype(BF), new_page(log_f[T:]).transpose(0, 2, 1))
    o_a_s = _headwise_mm(o_lat.reshape(Ts, MLA_HEADS * KV_LORA), w_uv_h, BF, 1024, "v_up")
    o_f_s = o_f_s.reshape(B, FOX_KV_HEADS, t_new, FOX_GROUP, FOX_HD).transpose(0, 2, 1, 3, 4).reshape(Ts, FOX_W)
    o_m_s = _mem_attend(mq[T:].astype(F32).reshape(B, t_new, MEM_W),
                        cache_mem_k[0].reshape(B, n_mem * MEM_HEADS, MEM_HD),
                        cache_mem_v[0].reshape(B, n_mem * MEM_HEADS, MEM_HD), t_new, F32, "mem_sample")
    o_m_s = o_m_s.reshape(Ts, MEM_W).astype(BF)

    merged = _merge(jnp.concatenate([o_a_p, o_a_s]), jnp.concatenate([o_f_p, o_f_s]),
                    jnp.concatenate([o_m_p, o_m_s]), hb,
                    w_br_mla[0].astype(BF), w_br_fox[0].astype(BF), w_br_mem[0].astype(BF))
    y = _out_ln(merged, w_out[0].astype(BF), x_all, ln_g[0], ln_b[0], alpha)

    mk = mkv[:, :MEM_W].reshape(1, 1, n_mem, MEM_HEADS, MEM_HD)
    mv = mkv[:, MEM_W:].reshape(1, 1, n_mem, MEM_HEADS, MEM_HD)
    return (
        y[:T][None], y[T:].reshape(B, t_new, D_MODEL),
        ckv[:T][None, None], krope[:T][None, None],
        fox_k[:T].reshape(1, 1, T, FOX_KV_HEADS, FOX_HD), fox_v[:T].reshape(1, 1, T, FOX_KV_HEADS, FOX_HD),
        log_f[:T][None, None], mk, mv,
        ckv[T:].reshape(1, B, t_new, KV_LORA), krope[T:].reshape(1, B, t_new, ROPE_HD),
        fox_k[T:].reshape(1, B, t_new, FOX_KV_HEADS, FOX_HD), fox_v[T:].reshape(1, B, t_new, FOX_KV_HEADS, FOX_HD),
        log_f[T:].reshape(1, B, t_new, FOX_HEADS),
    )
```
